```python
import math
import jax, jax.numpy as jnp
from jax import lax
import numpy as np

D_MODEL = 1024
BATCH = 16
SEQ = 256
DEPTH = 4
DEC_BATCH = 4
DEC_SEQ = 2048
PAST_LEN = 512

F32 = jnp.float32
GRID_W = 64
N_MIXERS = 3
N_CONV = (DEPTH + 2) // 3
N_RWKV = (DEPTH + 1) // 3
N_ATTN = DEPTH // 3
ADA_CHUNKS = 6
NORM_EPS = 1e-6
LN_EPS = 1e-5
CONV_WIDTH = 31
RWKV_HEAD = 64
RWKV_HEADS = D_MODEL // RWKV_HEAD
DECAY_LORA = 64
ICLR_LORA = 64
GATE_LORA = 160
GN_EPS = 64e-5
L2_EPS = 1e-12
HEAD_DIM = 64
N_HEADS = D_MODEL // HEAD_DIM
N_KV_HEADS = 4
GROUP = N_HEADS // N_KV_HEADS
WINDOW = 128
ATTN_BLOCK = 128
ATTN_SCALE = HEAD_DIM ** -0.5
ROPE_THETA = 10000.0
ROPE_PAIRS = HEAD_DIM // 4
NEG_INF = -1e30
N_EXPERTS = 32
TOP_K = 4
D_FF_EXPERT = D_MODEL
SWIGLU_LIMIT = 7.0
SWIGLU_ALPHA = 1.702
MOE_BLOCK = 128

kernel_name = 'hybrid_dit_conv_rwkv7_swa_moe_step'


def rmsnorm(x, g):
    xf = x.astype(F32)
    y = xf * lax.rsqrt(jnp.mean(xf * xf, axis=-1, keepdims=True) + NORM_EPS)
    return y.astype(x.dtype) * g


def layernorm(x, g, b):
    xf = x.astype(F32)
    mu = jnp.mean(xf, axis=-1, keepdims=True)
    var = jnp.mean(jnp.square(xf - mu), axis=-1, keepdims=True)
    return ((xf - mu) * lax.rsqrt(var + LN_EPS)).astype(x.dtype) * g + b


def adaln(cond, w_ada, b_ada):
    m = jax.nn.silu(cond) @ w_ada + b_ada
    return m.reshape(cond.shape[0], ADA_CHUNKS, 1, D_MODEL)


def conformer_conv(h, w_in, b_in, w_dw, b_dw, ln_g, ln_b, w_out, b_out):
    d = h.shape[-1]
    u = h @ w_in + b_in
    u = u[..., :d] * jax.nn.sigmoid(u[..., d:])
    u = lax.conv_general_dilated(u, w_dw[:, None, :], window_strides=(1,),
                                 padding=[(CONV_WIDTH // 2, CONV_WIDTH // 2)],
                                 dimension_numbers=('NWC', 'WIO', 'NWC'),
                                 feature_group_count=d) + b_dw
    u = jax.nn.silu(layernorm(u, ln_g, ln_b))
    return u @ w_out + b_out


def wkv7_scan(s0, r, w, k, v, a, b, reverse):
    seq = tuple(jnp.swapaxes(z, 0, 1) for z in (r, w, k, v, a, b))

    def step(S, inp):
        rt, wt, kt, vt, at, bt = inp
        sa = jnp.einsum('bhvk,bhk->bhv', S, at)
        S = S * wt[:, :, None, :] + sa[..., None] * bt[:, :, None, :] + vt[..., None] * kt[:, :, None, :]
        return S, jnp.einsum('bhvk,bhk->bhv', S, rt)

    s_fin, y = lax.scan(step, s0.astype(F32), seq, reverse=reverse)
    return s_fin, jnp.swapaxes(y, 0, 1)


def rwkv7_mix(h, s0_f, s0_b, mu, w_r, w_k, w_v, w_o, w0, w1, w2, a0, a1, a2, g1, g2,
              k_k, k_a, r_k, lnx_g, lnx_b):
    bsz, t, d = h.shape

    def heads(z):
        return z.astype(F32).reshape(bsz, t, RWKV_HEADS, RWKV_HEAD)

    x_prev = jnp.pad(h, ((0, 0), (1, 0), (0, 0)))[:, :t]
    x_next = jnp.pad(h, ((0, 0), (0, 1), (0, 0)))[:, 1:]
    xx = 0.5 * (x_prev + x_next) - h
    xr, xw, xk, xv, xa, xg = [h + xx * mu[i] for i in range(6)]
    r = heads(xr @ w_r)
    k = xk @ w_k
    v = heads(xv @ w_v)
    g = jax.nn.sigmoid(xg @ g1) @ g2
    kk = heads(k * k_k)
    kk = kk * lax.rsqrt(jnp.sum(kk * kk, axis=-1, keepdims=True) + L2_EPS)

    def direction(dix, s0, reverse):
        w_log = -jax.nn.softplus(-(w0[dix] + jnp.tanh(xw @ w1[dix]) @ w2[dix])) - 0.5
        decay = jnp.exp(-jnp.exp(heads(w_log)))
        a = jax.nn.sigmoid(a0[dix] + (xa @ a1[dix]) @ a2[dix])
        kd = heads(k * (1 + (a - 1) * k_a))
        s_fin, y = wkv7_scan(s0, r, decay, kd, v, -kk, kk * heads(a), reverse)
        bonus = jnp.sum(r * kd * r_k.astype(F32), axis=-1, keepdims=True) * v
        return s_fin, y, bonus

    s_f, y_f, bonus_f = direction(0, s0_f, False)
    s_b, y_b, bonus_b = direction(1, s0_b, True)
    y = y_f + y_b
    mean = jnp.mean(y, axis=-1, keepdims=True)
    var = jnp.mean(jnp.square(y - mean), axis=-1, keepdims=True)
    y = ((y - mean) * lax.rsqrt(var + GN_EPS)).reshape(bsz, t, d) * lnx_g.astype(F32) + lnx_b.astype(F32)
    y = (y + (bonus_f + bonus_b).reshape(bsz, t, d)).astype(h.dtype) * g
    return y @ w_o, s_f, s_b


def qkv_proj(h, w_qkv):
    bsz, t, _ = h.shape
    qkv = h @ w_qkv
    nq = N_HEADS * HEAD_DIM
    nk = N_KV_HEADS * HEAD_DIM
    q = qkv[..., :nq].reshape(bsz, t, N_HEADS, HEAD_DIM)
    k = qkv[..., nq:nq + nk].reshape(bsz, t, N_KV_HEADS, HEAD_DIM)
    v = qkv[..., nq + nk:].reshape(bsz, t, N_KV_HEADS, HEAD_DIM)
    return q, k, v


def axial_rope_tables(t, dtype):
    rows = t // GRID_W
    row = jnp.repeat(jnp.arange(rows, dtype=F32), GRID_W)
    col = jnp.tile(jnp.arange(GRID_W, dtype=F32), rows)
    inv = ROPE_THETA ** (-jnp.arange(ROPE_PAIRS, dtype=F32) / ROPE_PAIRS)
    ang_r = (row[:, None] * inv)[:, None, :]
    ang_c = (col[:, None] * inv)[:, None, :]
    return tuple(z.astype(dtype) for z in (jnp.cos(ang_r), jnp.sin(ang_r), jnp.cos(ang_c), jnp.sin(ang_c)))


def rotate(x, cos, sin):
    x1, x2 = x[..., :ROPE_PAIRS], x[..., ROPE_PAIRS:]
    return jnp.concatenate([x1 * cos - x2 * sin, x2 * cos + x1 * sin], axis=-1)


def apply_axial_rope(x, tabs):
    cr, sr, cc, sc = tabs
    half = HEAD_DIM // 2
    return jnp.concatenate([rotate(x[..., :half], cr, sr), rotate(x[..., half:], cc, sc)], axis=-1)


def context_attention(h, w_qkv, w_o, sinks):
    bsz, s, _ = h.shape
    q, k, v = qkv_proj(h, w_qkv)
    qg = q.reshape(bsz, s, N_KV_HEADS, GROUP, HEAD_DIM)
    sc = jnp.einsum('bqkgd,bskd->bkgqs', qg, k).astype(F32) * ATTN_SCALE
    sink = jnp.broadcast_to(sinks.astype(F32).reshape(1, N_KV_HEADS, GROUP, 1, 1), sc.shape[:-1] + (1,))
    p = jax.nn.softmax(jnp.concatenate([sc, sink], axis=-1), axis=-1)[..., :s]
    o = jnp.einsum('bkgqs,bskd->bqkgd', p.astype(v.dtype), v).reshape(bsz, s, N_HEADS * HEAD_DIM)
    return o @ w_o, k, v


def latent_attention(h, ck, cv, w_qkv, w_o, sinks):
    bsz, t, _ = h.shape
    n_blk = t // ATTN_BLOCK
    n_ctx = ck.shape[1]
    n_loc = 3 * ATTN_BLOCK
    q, k, v = qkv_proj(h, w_qkv)
    tabs = axial_rope_tables(t, h.dtype)
    q = apply_axial_rope(q, tabs)
    k = apply_axial_rope(k, tabs)
    qb = q.reshape(bsz, n_blk, ATTN_BLOCK, N_KV_HEADS, GROUP, HEAD_DIM)
    pad = ((0, 0), (ATTN_BLOCK, ATTN_BLOCK), (0, 0), (0, 0))
    kp = jnp.pad(k, pad).reshape(bsz, n_blk + 2, ATTN_BLOCK, N_KV_HEADS, HEAD_DIM)
    vp = jnp.pad(v, pad).reshape(bsz, n_blk + 2, ATTN_BLOCK, N_KV_HEADS, HEAD_DIM)
    kw = jnp.concatenate([kp[:, :-2], kp[:, 1:-1], kp[:, 2:]], axis=2)
    vw = jnp.concatenate([vp[:, :-2], vp[:, 1:-1], vp[:, 2:]], axis=2)
    s_loc = jnp.einsum('bnqkgd,bnskd->bnkgqs', qb, kw).astype(F32) * ATTN_SCALE
    q_off = jnp.arange(ATTN_BLOCK)[:, None]
    s_off = jnp.arange(n_loc)[None, :] - ATTN_BLOCK
    band = jnp.abs(s_off - q_off) <= WINDOW
    key_pos = jnp.arange(n_blk)[:, None] * ATTN_BLOCK + s_off
    valid = band[None] & ((key_pos >= 0) & (key_pos < t))[:, None, :]
    s_loc = jnp.where(valid[None, :, None, None], s_loc, NEG_INF)
    s_ctx = jnp.einsum('bnqkgd,bckd->bnkgqc', qb, ck).astype(F32) * ATTN_SCALE
    sink = jnp.broadcast_to(sinks.astype(F32).reshape(1, 1, N_KV_HEADS, GROUP, 1, 1), s_loc.shape[:-1] + (1,))
    p = jax.nn.softmax(jnp.concatenate([s_loc, s_ctx, sink], axis=-1), axis=-1)
    o = (jnp.einsum('bnkgqs,bnskd->bnqkgd', p[..., :n_loc].astype(vw.dtype), vw)
         + jnp.einsum('bnkgqc,bckd->bnqkgd', p[..., n_loc:n_loc + n_ctx].astype(cv.dtype), cv))
    return o.reshape(bsz, t, N_HEADS * HEAD_DIM) @ w_o


def moe_ffn(h, router_w, router_b, w_gu, b_gu, w_down, b_down):
    bsz, t, d = h.shape
    x = h.reshape(bsz * t, d)
    n_assign = bsz * t * TOP_K
    logits = (x @ router_w).astype(F32) + router_b.astype(F32)
    top_val, top_idx = lax.top_k(logits, TOP_K)
    gates = jax.nn.softmax(top_val, axis=-1)
    flat_e = top_idx.reshape(n_assign)
    flat_tok = jnp.arange(n_assign, dtype=jnp.int32) // TOP_K
    flat_g = gates.reshape(n_assign)
    order = jnp.argsort(flat_e, stable=True)
    e_sorted = flat_e[order]
    counts = jnp.bincount(flat_e, length=N_EXPERTS)
    starts = jnp.cumsum(counts) - counts
    padded = (counts + MOE_BLOCK - 1) // MOE_BLOCK * MOE_BLOCK
    padded_ends = jnp.cumsum(padded)
    padded_starts = padded_ends - padded
    dest = padded_starts[e_sorted] + jnp.arange(n_assign) - starts[e_sorted]
    n_blocks = -(-(n_assign + N_EXPERTS * (MOE_BLOCK - 1)) // MOE_BLOCK)
    n_slots = n_blocks * MOE_BLOCK
    slot_tok = jnp.zeros((n_slots,), jnp.int32).at[dest].set(flat_tok[order])
    slot_gate = jnp.zeros((n_slots,), F32).at[dest].set(flat_g[order])
    block_expert = jnp.minimum(
        jnp.searchsorted(padded_ends, jnp.arange(n_blocks) * MOE_BLOCK, side='right'), N_EXPERTS - 1)
    xb = x[slot_tok].reshape(n_blocks, MOE_BLOCK, d)

    def expert_block(args):
        xe, e = args
        gu = xe @ w_gu[e] + b_gu[e]
        gate = jnp.minimum(gu[..., :D_FF_EXPERT], SWIGLU_LIMIT)
        up = jnp.clip(gu[..., D_FF_EXPERT:], -SWIGLU_LIMIT, SWIGLU_LIMIT)
        glu = gate * jax.nn.sigmoid(gate * SWIGLU_ALPHA)
        return ((up + 1) * glu) @ w_down[e] + b_down[e]

    yb = lax.map(expert_block, (xb, block_expert))
    y = jnp.zeros_like(x).at[slot_tok].add(yb.reshape(n_slots, d) * slot_gate[:, None].astype(x.dtype))
    return y.reshape(bsz, t, d)


def setup_inputs(seed: int = 0) -> dict:
    key = jax.random.key(seed)
    keys = iter(jax.random.split(key, 64))

    def nrm(shape, scale):
        return jax.random.normal(next(keys), shape, F32) * scale

    def gain(shape):
        return 1.0 + nrm(shape, 0.02)

    D = D_MODEL
    H, N = RWKV_HEADS, RWKV_HEAD
    qkv_w = (N_HEADS + 2 * N_KV_HEADS) * HEAD_DIM
    return {
        'x_prompt': nrm((BATCH, SEQ, D), 1.0),
        'x_sample': nrm((DEC_BATCH, DEC_SEQ, D), 1.0),
        'state_wkv_fwd': nrm((DEC_BATCH, N_RWKV, H, N, N), 1.0),
        'state_wkv_bwd': nrm((DEC_BATCH, N_RWKV, H, N, N), 1.0),
        'cache_k': nrm((DEC_BATCH, N_ATTN, PAST_LEN, N_KV_HEADS, HEAD_DIM), 1.0),
        'cache_v': nrm((DEC_BATCH, N_ATTN, PAST_LEN, N_KV_HEADS, HEAD_DIM), 1.0),
        'c': nrm((DEC_BATCH, D), 1.0),
        'c_ctx': nrm((D,), 1.0),
        'ada_w': nrm((DEPTH, D, ADA_CHUNKS * D), 0.5 * D ** -0.5),
        'ada_b': nrm((DEPTH, ADA_CHUNKS * D), 0.02),
        'norm1_g': gain((DEPTH, D)),
        'norm2_g': gain((DEPTH, D)),
        'final_g': gain((D,)),
        'conv_w_in': nrm((N_CONV, D, 2 * D), D ** -0.5),
        'conv_b_in': nrm((N_CONV, 2 * D), 0.02),
        'conv_w_dw': nrm((N_CONV, CONV_WIDTH, D), CONV_WIDTH ** -0.5),
        'conv_b_dw': nrm((N_CONV, D), 0.02),
        'conv_ln_g': gain((N_CONV, D)),
        'conv_ln_b': nrm((N_CONV, D), 0.02),
        'conv_w_out': nrm((N_CONV, D, D), D ** -0.5),
        'conv_b_out': nrm((N_CONV, D), 0.02),
        'rwkv_mu': jax.random.uniform(next(keys), (N_RWKV, 6, D), F32, 0.0, 1.0),
        'rwkv_w_r': nrm((N_RWKV, D, D), D ** -0.5),
        'rwkv_w_k': nrm((N_RWKV, D, D), D ** -0.5),
        'rwkv_w_v': nrm((N_RWKV, D, D), D ** -0.5),
        'rwkv_w_o': nrm((N_RWKV, D, D), D ** -0.5),
        'rwkv_w0': jax.random.uniform(next(keys), (N_RWKV, 2, D), F32, -6.0, -1.0),
        'rwkv_w1': nrm((N_RWKV, 2, D, DECAY_LORA), D ** -0.5),
        'rwkv_w2': nrm((N_RWKV, 2, DECAY_LORA, D), 0.5 * DECAY_LORA ** -0.5),
        'rwkv_a0': nrm((N_RWKV, 2, D), 0.1),
        'rwkv_a1': nrm((N_RWKV, 2, D, ICLR_LORA), D ** -0.5),
        'rwkv_a2': nrm((N_RWKV, 2, ICLR_LORA, D), 0.5 * ICLR_LORA ** -0.5),
        'rwkv_g1': nrm((N_RWKV, D, GATE_LORA), D ** -0.5),
        'rwkv_g2': nrm((N_RWKV, GATE_LORA, D), GATE_LORA ** -0.5),
        'rwkv_k_k': 0.85 + nrm((N_RWKV, D), 0.02),
        'rwkv_k_a': gain((N_RWKV, D)),
        'rwkv_r_k': nrm((N_RWKV, H, N), 0.1),
        'rwkv_lnx_g': gain((N_RWKV, D)),
        'rwkv_lnx_b': nrm((N_RWKV, D), 0.02),
        'attn_w_qkv': nrm((N_ATTN, D, qkv_w), D ** -0.5),
        'attn_w_o': nrm((N_ATTN, N_HEADS * HEAD_DIM, D), (N_HEADS * HEAD_DIM) ** -0.5),
        'attn_sinks': nrm((N_ATTN, N_HEADS), 0.5),
        'moe_router_w': nrm((DEPTH, D, N_EXPERTS), D ** -0.5),
        'moe_router_b': nrm((DEPTH, N_EXPERTS), 0.01),
        'moe_w_gu': nrm((DEPTH, N_EXPERTS, D, 2 * D_FF_EXPERT), D ** -0.5),
        'moe_b_gu': nrm((DEPTH, N_EXPERTS, 2 * D_FF_EXPERT), 0.02),
        'moe_w_down': nrm((DEPTH, N_EXPERTS, D_FF_EXPERT, D), D_FF_EXPERT ** -0.5),
        'moe_b_down': nrm((DEPTH, N_EXPERTS, D), 0.02),
    }


def reference(x_prompt, x_sample, state_wkv_fwd, state_wkv_bwd, cache_k, cache_v, c, c_ctx,
              ada_w, ada_b, norm1_g, norm2_g, final_g,
              conv_w_in, conv_b_in, conv_w_dw, conv_b_dw, conv_ln_g, conv_ln_b, conv_w_out, conv_b_out,
              rwkv_mu, rwkv_w_r, rwkv_w_k, rwkv_w_v, rwkv_w_o, rwkv_w0, rwkv_w1, rwkv_w2,
              rwkv_a0, rwkv_a1, rwkv_a2, rwkv_g1, rwkv_g2, rwkv_k_k, rwkv_k_a, rwkv_r_k,
              rwkv_lnx_g, rwkv_lnx_b,
              attn_w_qkv, attn_w_o, attn_sinks,
              moe_router_w, moe_router_b, moe_w_gu, moe_b_gu, moe_w_down, moe_b_down):
    xp, xs = x_prompt, x_sample
    bsz_p = x_prompt.shape[0]
    new_f, new_b, new_k, new_v = [], [], [], []
    ci = ri = ai = 0
    for layer in range(DEPTH):
        mp = adaln(c_ctx[None], ada_w[layer], ada_b[layer])
        ms = adaln(c, ada_w[layer], ada_b[layer])
        hp = rmsnorm(xp, norm1_g[layer]) * (1 + mp[:, 1]) + mp[:, 0]
        hs = rmsnorm(xs, norm1_g[layer]) * (1 + ms[:, 1]) + ms[:, 0]
        kind = layer % N_MIXERS
        if kind == 0:
            cp = (conv_w_in[ci], conv_b_in[ci], conv_w_dw[ci], conv_b_dw[ci],
                  conv_ln_g[ci], conv_ln_b[ci], conv_w_out[ci], conv_b_out[ci])
            op = conformer_conv(hp, *cp)
            os_ = conformer_conv(hs, *cp)
            ci += 1
        elif kind == 1:
            rp = (rwkv_mu[ri], rwkv_w_r[ri], rwkv_w_k[ri], rwkv_w_v[ri], rwkv_w_o[ri],
                  rwkv_w0[ri], rwkv_w1[ri], rwkv_w2[ri], rwkv_a0[ri], rwkv_a1[ri], rwkv_a2[ri],
                  rwkv_g1[ri], rwkv_g2[ri], rwkv_k_k[ri], rwkv_k_a[ri], rwkv_r_k[ri],
                  rwkv_lnx_g[ri], rwkv_lnx_b[ri])
            zero_state = jnp.zeros((bsz_p, RWKV_HEADS, RWKV_HEAD, RWKV_HEAD), F32)
            op, s_f, s_b = rwkv7_mix(hp, zero_state, zero_state, *rp)
            os_, _, _ = rwkv7_mix(hs, state_wkv_fwd[:, ri], state_wkv_bwd[:, ri], *rp)
            new_f.append(s_f.astype(x_prompt.dtype))
            new_b.append(s_b.astype(x_prompt.dtype))
            ri += 1
        else:
            op, k_ctx, v_ctx = context_attention(hp, attn_w_qkv[ai], attn_w_o[ai], attn_sinks[ai])
            os_ = latent_attention(hs, cache_k[:, ai], cache_v[:, ai], attn_w_qkv[ai], attn_w_o[ai], attn_sinks[ai])
            new_k.append(k_ctx)
            new_v.append(v_ctx)
            ai += 1
        xp = xp + mp[:, 2] * op
        xs = xs + ms[:, 2] * os_
        mo = (moe_router_w[layer], moe_router_b[layer], moe_w_gu[layer], moe_b_gu[layer],
              moe_w_down[layer], moe_b_down[layer])
        hp = rmsnorm(xp, norm2_g[layer]) * (1 + mp[:, 4]) + mp[:, 3]
        hs = rmsnorm(xs, norm2_g[layer]) * (1 + ms[:, 4]) + ms[:, 3]
        xp = xp + mp[:, 5] * moe_ffn(hp, *mo)
        xs = xs + ms[:, 5] * moe_ffn(hs, *mo)
    y_prompt = rmsnorm(xp, final_g)
    y_sample = rmsnorm(xs, final_g)
    new_state_wkv_fwd = jnp.stack(new_f, axis=1)
    new_state_wkv_bwd = jnp.stack(new_b, axis=1)
    new_cache_k = jnp.stack(new_k, axis=1)
    new_cache_v = jnp.stack(new_v, axis=1)
    return (y_prompt, y_sample, new_state_wkv_fwd, new_state_wkv_bwd, new_cache_k, new_cache_v)
```

```python
import functools

import numpy as np
import jax
import jax.numpy as jnp
from jax import lax
from jax.experimental import pallas as pl
from jax.experimental.pallas import tpu as pltpu

F32 = jnp.float32
BF16 = jnp.bfloat16

D = 1024
DEPTH = 4
P_BATCH, P_SEQ = 16, 256
S_BATCH, S_SEQ = 4, 2048
PAST = 512
GRID_W = 64
N_PROMPT = P_BATCH * P_SEQ
N_SAMPLE = S_BATCH * S_SEQ
N_TOK = N_PROMPT + N_SAMPLE
ADA_CHUNKS = 6
NORM_EPS = 1e-6
LN_EPS = 1e-5
CONV_WIDTH = 31
CONV_HALF = CONV_WIDTH // 2
HEAD = 64
N_HEAD = D // HEAD
GN_EPS = 64e-5
L2_EPS = 1e-12
N_KV = 4
GROUP = N_HEAD // N_KV
WINDOW = 128
ATTN_SCALE = HEAD ** -0.5
ROPE_THETA = 10000.0
ROPE_PAIRS = HEAD // 4
NEG_INF = -1e30
N_EXPERTS = 32
TOP_K = 4
D_FF = D
SWIGLU_LIMIT = 7.0
SWIGLU_ALPHA = 1.702

LANE = 128
SUBLANE = 8
MXU_DIM = 256
VMEM_LIMIT = 56 * 1024 * 1024

RB = 256
N_RB = N_TOK // RB
P_RB = N_PROMPT // RB
S_RB = S_SEQ // RB
N_COND = 8
CONV_HALO = 16
SHIFT_HALO = SUBLANE
CHUNK = 64
N_CHUNK = N_TOK // CHUNK
HG = MXU_DIM
N_HG = D // HG
HEADS_PER_HG = HG // HEAD
N_SEQ = P_BATCH + S_BATCH
QB = 128
TME = 256
N_ASSIGN = N_TOK * TOP_K
N_EBLOCKS = -(-(N_ASSIGN + N_EXPERTS * (TME - 1)) // TME)
N_SLOTS = N_EBLOCKS * TME

NN = ((1,), (0,))
NT = ((1,), (1,))
TN = ((0,), (0,))


def _dot(a, b, dims=NN):
    return lax.dot_general(a, b, (dims, ((), ())), preferred_element_type=F32)


def _dot1(a, b, dims=NN):
    return _dot(a.astype(BF16), b.astype(BF16), dims)


def _split2(x):
    hi = x.astype(BF16)
    return hi, (x - hi.astype(F32)).astype(BF16)


def _dot3(a, b, dims=NN):
    ah, al = _split2(a)
    bh, bl = _split2(b)
    return _dot(ah, bh, dims) + (_dot(ah, bl, dims) + _dot(al, bh, dims))


def _sigmoid(x):
    return 1.0 / (1.0 + jnp.exp(-x))


def _rms_mod(x, g, shift, scale):
    ms = jnp.mean(x * x, axis=-1, keepdims=True)
    return (x * lax.rsqrt(ms + NORM_EPS) * g) * (1.0 + scale) + shift


def _cond_of_block(i):
    return jnp.where(i < P_RB, 0, 1 + (i - P_RB) // S_RB)


def _seq_pos(i):
    j = jnp.where(i < P_RB, 0, (i - P_RB) % S_RB)
    first = (i < P_RB) | (j == 0)
    last = (i < P_RB) | (j == S_RB - 1)
    return first, last


def _row_spec(width=D, rows=RB):
    return pl.BlockSpec((rows, width), lambda i: (i, 0))


def _mod_spec(chunk):
    return pl.BlockSpec((None, 1, D), lambda i: (_cond_of_block(i) * ADA_CHUNKS + chunk, 0, 0))


def _full_spec(shape):
    nd = len(shape)
    return pl.BlockSpec(shape, lambda i: (0,) * nd)


def _params(n_axes=1):
    return pltpu.CompilerParams(dimension_semantics=("arbitrary",) * n_axes,
                                vmem_limit_bytes=VMEM_LIMIT)


def _row(v):
    return v.reshape(1, -1)


def _ada_kernel(c_ref, w_ref, b_ref, o_ref):
    c = c_ref[...]
    o_ref[...] = _dot3(c * _sigmoid(c), w_ref[...]) + b_ref[...]


def _ada_table(cond, ada_w, ada_b):
    out = pl.pallas_call(
        _ada_kernel,
        grid=(DEPTH, ADA_CHUNKS),
        in_specs=[pl.BlockSpec((N_COND, D), lambda l, j: (0, 0)),
                  pl.BlockSpec((None, D, D), lambda l, j: (l, 0, j)),
                  pl.BlockSpec((None, 1, D), lambda l, j: (l, 0, j))],
        out_specs=pl.BlockSpec((None, N_COND, D), lambda l, j: (l, 0, j)),
        out_shape=jax.ShapeDtypeStruct((DEPTH, N_COND, ADA_CHUNKS * D), F32),
        compiler_params=_params(2),
        name="ada_table",
    )(cond, ada_w, ada_b.reshape(DEPTH, 1, ADA_CHUNKS * D))
    return out.reshape(DEPTH, N_COND * ADA_CHUNKS, 1, D)


def _conv_in_kernel(x_ref, g_ref, sh_ref, sc_ref, w_ref, b_ref, u_ref):
    h = _rms_mod(x_ref[...], g_ref[...], sh_ref[...], sc_ref[...]).astype(BF16)
    z = _dot(h, w_ref[...]) + b_ref[...]
    u_ref[...] = z[:, :D] * _sigmoid(z[:, D:])


def _conv_out_kernel(up_ref, uc_ref, un_ref, x_ref, wdw_ref, bdw_ref, lg_ref, lb_ref, wo_ref,
                     bo_ref, gate_ref, o_ref, ext_ref, act_ref):
    first, last = _seq_pos(pl.program_id(0))
    ext_ref[0:CONV_HALO, :] = jnp.where(first, 0.0, up_ref[...])
    ext_ref[CONV_HALO:CONV_HALO + RB, :] = uc_ref[...]
    ext_ref[CONV_HALO + RB:, :] = jnp.where(last, 0.0, un_ref[...])
    rows = 128
    base = CONV_HALO - CONV_HALF
    for ct in range(D // LANE):
        lanes = slice(ct * LANE, (ct + 1) * LANE)
        for rc in range(RB // rows):
            acc = jnp.broadcast_to(bdw_ref[:, lanes], (rows, LANE))
            for t in range(CONV_WIDTH):
                r0 = rc * rows + base + t
                acc = acc + ext_ref[r0:r0 + rows, lanes] * wdw_ref[t:t + 1, lanes]
            act_ref[rc * rows:(rc + 1) * rows, lanes] = acc
    c = act_ref[...]
    mu = jnp.mean(c, axis=-1, keepdims=True)
    cc = c - mu
    var = jnp.mean(cc * cc, axis=-1, keepdims=True)
    y = cc * lax.rsqrt(var + LN_EPS) * lg_ref[...] + lb_ref[...]
    y = y * _sigmoid(y)
    out = _dot(y.astype(BF16), wo_ref[...]) + bo_ref[...]
    o_ref[...] = x_ref[...] + gate_ref[...] * out


def _conformer_layer(x, mods, g1, w_in, b_in, w_dw, b_dw, ln_g, ln_b, w_out, b_out):
    u = pl.pallas_call(
        _conv_in_kernel,
        grid=(N_RB,),
        in_specs=[_row_spec(), _full_spec((1, D)), _mod_spec(0), _mod_spec(1),
                  _full_spec((D, 2 * D)), _full_spec((1, 2 * D))],
        out_specs=_row_spec(),
        out_shape=jax.ShapeDtypeStruct((N_TOK, D), F32),
        compiler_params=_params(),
        name="conv_in",
    )(x, _row(g1), mods, mods, w_in.astype(BF16), _row(b_in))
    hb = RB // CONV_HALO
    n_hb = N_TOK // CONV_HALO
    wdw = jnp.zeros((32, D), F32).at[:CONV_WIDTH].set(w_dw)
    return pl.pallas_call(
        _conv_out_kernel,
        grid=(N_RB,),
        in_specs=[pl.BlockSpec((CONV_HALO, D), lambda i: (jnp.maximum(i * hb - 1, 0), 0)),
                  _row_spec(),
                  pl.BlockSpec((CONV_HALO, D), lambda i: (jnp.minimum((i + 1) * hb, n_hb - 1), 0)),
                  _row_spec(), _full_spec((32, D)), _full_spec((1, D)), _full_spec((1, D)),
                  _full_spec((1, D)), _full_spec((D, D)), _full_spec((1, D)), _mod_spec(2)],
        out_specs=_row_spec(),
        out_shape=jax.ShapeDtypeStruct((N_TOK, D), F32),
        scratch_shapes=[pltpu.VMEM((RB + 2 * CONV_HALO, D), F32), pltpu.VMEM((RB, D), F32)],
        compiler_params=_params(),
        name="conv_out",
    )(u, u, u, x, wdw, _row(b_dw), _row(ln_g), _row(ln_b), w_out.astype(BF16), _row(b_out), mods)


def _softplus(z):
    return jnp.maximum(z, 0.0) + jnp.log(1.0 + jnp.exp(-jnp.abs(z)))


def _head_pair_sums(x):
    left = lax.broadcasted_iota(jnp.int32, x.shape, 1) < HEAD
    sl = jnp.sum(jnp.where(left, x, 0.0), axis=-1, keepdims=True)
    sr = jnp.sum(jnp.where(left, 0.0, x), axis=-1, keepdims=True)
    return jnp.where(left, sl, sr)


def _rwkv_in_kernel(xp_ref, xc_ref, xn_ref, g_ref, sh_ref, sc_ref, mu_ref, wr_ref, wk_ref, wv_ref,
                    w1_ref, w2_ref, w0_ref, a1_ref, a2_ref, a0_ref, g1_ref, g2_ref, kkk_ref,
                    r_o, k_o, v_o, g_o, kk_o, wl0_o, wl1_o, as0_o, as1_o, ext_ref):
    first, last = _seq_pos(pl.program_id(0))
    g, sh, sc = g_ref[...], sh_ref[...], sc_ref[...]
    h = _rms_mod(xc_ref[...], g, sh, sc)
    hp = _rms_mod(xp_ref[SHIFT_HALO - 1:SHIFT_HALO, :], g, sh, sc)
    hn = _rms_mod(xn_ref[0:1, :], g, sh, sc)
    ext_ref[SHIFT_HALO - 1:SHIFT_HALO, :] = jnp.where(first, 0.0, hp)
    ext_ref[SHIFT_HALO:SHIFT_HALO + RB, :] = h
    ext_ref[SHIFT_HALO + RB:SHIFT_HALO + RB + 1, :] = jnp.where(last, 0.0, hn)
    xx = 0.5 * (ext_ref[SHIFT_HALO - 1:SHIFT_HALO - 1 + RB, :]
                + ext_ref[SHIFT_HALO + 1:SHIFT_HALO + 1 + RB, :]) - h

    def mix(j):
        return (h + xx * mu_ref[j:j + 1, :]).astype(BF16)

    r_o[...] = _dot(mix(0), wr_ref[...])
    k = _dot(mix(2), wk_ref[...])
    k_o[...] = k
    v_o[...] = _dot(mix(3), wv_ref[...])
    t1 = jnp.tanh(_dot(mix(1), w1_ref[...]))
    wl = w0_ref[...] + _dot(t1.astype(BF16), w2_ref[...])
    wlog = -_softplus(-wl) - 0.5
    wl0_o[...] = wlog[:, :D]
    wl1_o[...] = wlog[:, D:]
    ah = _dot(mix(4), a1_ref[...])
    asig = _sigmoid(a0_ref[...] + _dot(ah.astype(BF16), a2_ref[...]))
    as0_o[...] = asig[:, :D]
    as1_o[...] = asig[:, D:]
    gh = _sigmoid(_dot(mix(5), g1_ref[...]))
    g_o[...] = _dot(gh.astype(BF16), g2_ref[...])
    kr = k * kkk_ref[...]
    for ct in range(D // LANE):
        lanes = slice(ct * LANE, (ct + 1) * LANE)
        t = kr[:, lanes]
        kk_o[:, lanes] = t * lax.rsqrt(_head_pair_sums(t * t) + L2_EPS)


def _block_diag(x, mask):
    return jnp.where(mask, jnp.concatenate([x] * HEADS_PER_HG, axis=0), 0.0)


def _wkv_group(S, rt, at, bt, kt, v, p_end, reverse):
    row = lax.broadcasted_iota(jnp.int32, (CHUNK, HG), 0)
    col = lax.broadcasted_iota(jnp.int32, (CHUNK, HG), 1) % CHUNK
    if reverse:
        strict, incl = row < col, row <= col
    else:
        strict, incl = row > col, row >= col
    bdm = (lax.broadcasted_iota(jnp.int32, (HG, HG), 0) // CHUNK
           == lax.broadcasted_iota(jnp.int32, (HG, HG), 1) // CHUNK)
    bd = functools.partial(_block_diag, mask=bdm)

    def same_block(n):
        return (row // n) == (col // n)

    kb, kkb, vb = bd(bt), bd(kt), bd(v)
    a_ab = jnp.where(strict, _dot1(at, kb, NT), 0.0)
    a_ak = jnp.where(strict, _dot1(at, kkb, NT), 0.0)
    a_rb = jnp.where(incl, _dot1(rt, kb, NT), 0.0)
    a_rk = jnp.where(incl, _dot1(rt, kkb, NT), 0.0)
    wm = _dot1(at, S, NT) + _dot1(a_ak, vb)
    n1 = jnp.where(same_block(8), a_ab, 0.0)
    t = jnp.where(row == col, 1.0, 0.0) + n1
    n2 = _dot3(n1, bd(n1))
    t = t + _dot3(n2, bd(t))
    n4 = _dot3(n2, bd(n2))
    t = t + _dot3(n4, bd(t))
    for n in (16, 32):
        m = jnp.where(same_block(n) & ~same_block(n // 2), a_ab, 0.0)
        t = t + _dot1(t, bd(_dot1(m, bd(t))))
    m = jnp.where(same_block(32), 0.0, a_ab)
    x = _dot1(t, bd(wm))
    u = x + _dot1(t, bd(_dot1(m, bd(x))))
    y = _dot1(rt, S, NT) + _dot1(a_rb, bd(u)) + _dot1(a_rk, vb)
    uv = jnp.concatenate([u, v], axis=0)
    bk = jnp.concatenate([bt, kt], axis=0)
    s_new = (S + jnp.where(bdm, _dot1(uv, bk, TN), 0.0)) * p_end
    return s_new, y


def _wkv_kernel(cidx_ref, seq_ref, st_ref, en_ref, r_ref, k_ref, v_ref, kk_ref, wl_ref, a_ref,
                ka_ref, rk_ref, s0_ref, y_ref, bonus_ref, sf_ref, s_scr, *, reverse):
    del cidx_ref, seq_ref
    s = pl.program_id(0)
    bdm = (lax.broadcasted_iota(jnp.int32, (HG, HG), 0) // CHUNK
           == lax.broadcasted_iota(jnp.int32, (HG, HG), 1) // CHUNK)

    @pl.when(st_ref[s] == 1)
    def _():
        for gi in range(N_HG):
            s_scr[gi] = _block_diag(s0_ref[gi], bdm)

    lw = -jnp.exp(wl_ref[...])
    ti = lax.broadcasted_iota(jnp.int32, (CHUNK, CHUNK), 0)
    tj = lax.broadcasted_iota(jnp.int32, (CHUNK, CHUNK), 1)
    tri = jnp.where((ti <= tj) if reverse else (ti >= tj), 1.0, 0.0).astype(BF16)
    p0 = lw.astype(BF16)
    r1 = lw - p0.astype(F32)
    p1 = r1.astype(BF16)
    p2 = (r1 - p1.astype(F32)).astype(BF16)
    li = _dot(tri, p0) + (_dot(tri, p1) + _dot(tri, p2))
    e_incl = jnp.exp(li)
    e_neg = jnp.exp(-li)
    e_excl = jnp.exp(li - lw)
    p_end = jnp.exp(jnp.sum(lw, axis=0, keepdims=True))
    r, v, kk, asig = r_ref[...], v_ref[...], kk_ref[...], a_ref[...]
    kd = k_ref[...] * (1.0 + (asig - 1.0) * ka_ref[...])
    rt = r * e_incl
    at = -kk * e_excl
    bt = kk * asig * e_neg
    kt = kd * e_neg
    for gi in range(N_HG):
        lanes = slice(gi * HG, (gi + 1) * HG)
        s_new, y = _wkv_group(s_scr[gi], rt[:, lanes], at[:, lanes], bt[:, lanes], kt[:, lanes],
                              v[:, lanes], p_end[:, lanes], reverse)
        s_scr[gi] = s_new
        y_ref[:, lanes] = y
    q = r * kd * rk_ref[...]
    for ct in range(D // LANE):
        lanes = slice(ct * LANE, (ct + 1) * LANE)
        bonus_ref[:, lanes] = _head_pair_sums(q[:, lanes]) * v[:, lanes]

    @pl.when(en_ref[s] == 1)
    def _():
        for gi in range(N_HG):
            sg = s_scr[gi]
            acc = sg[0:HEAD]
            for j in range(1, HEADS_PER_HG):
                acc = acc + sg[j * HEAD:(j + 1) * HEAD]
            sf_ref[gi] = acc


def _chunk_tables(reverse):
    seq = np.concatenate([np.repeat(np.arange(P_BATCH), P_SEQ // CHUNK),
                          P_BATCH + np.repeat(np.arange(S_BATCH), S_SEQ // CHUNK)])
    cidx = np.arange(N_CHUNK)
    if reverse:
        cidx = cidx[::-1].copy()
    seq = seq[cidx]
    start = np.ones(N_CHUNK, np.int32)
    start[1:] = seq[1:] != seq[:-1]
    end = np.ones(N_CHUNK, np.int32)
    end[:-1] = seq[1:] != seq[:-1]
    return (jnp.asarray(cidx, jnp.int32), jnp.asarray(seq, jnp.int32),
            jnp.asarray(start, jnp.int32), jnp.asarray(end, jnp.int32))


def _wkv_scan(r, k, v, kk, wlog, asig, k_a, r_k, s0cat, reverse):
    tok = pl.BlockSpec((CHUNK, D), lambda s, cidx, seq, st, en: (cidx[s], 0))
    par = pl.BlockSpec((1, D), lambda s, cidx, seq, st, en: (0, 0))
    state = pl.BlockSpec((None, N_HG, HEAD, HG), lambda s, cidx, seq, st, en: (seq[s], 0, 0, 0))
    return pl.pallas_call(
        functools.partial(_wkv_kernel, reverse=reverse),
        grid_spec=pltpu.PrefetchScalarGridSpec(
            num_scalar_prefetch=4,
            grid=(N_CHUNK,),
            in_specs=[tok, tok, tok, tok, tok, tok, par, par, state],
            out_specs=[tok, tok, state],
            scratch_shapes=[pltpu.VMEM((N_HG, HG, HG), F32)]),
        out_shape=[jax.ShapeDtypeStruct((N_TOK, D), F32), jax.ShapeDtypeStruct((N_TOK, D), F32),
                   jax.ShapeDtypeStruct((N_SEQ, N_HG, HEAD, HG), F32)],
        compiler_params=_params(),
        name="wkv_bwd" if reverse else "wkv_fwd",
    )(*_chunk_tables(reverse), r, k, v, kk, wlog, asig, k_a, r_k, s0cat)


def _rwkv_out_kernel(yf_ref, yb_ref, bf_ref, bb_ref, g_ref, x_ref, lg_ref, lb_ref, wo_ref, gate_ref,
                     o_ref, act_ref):
    for ct in range(D // LANE):
        lanes = slice(ct * LANE, (ct + 1) * LANE)
        y = yf_ref[:, lanes] + yb_ref[:, lanes]
        mean = _head_pair_sums(y) * (1.0 / HEAD)
        yc = y - mean
        var = _head_pair_sums(yc * yc) * (1.0 / HEAD)
        yn = yc * lax.rsqrt(var + GN_EPS) * lg_ref[:, lanes] + lb_ref[:, lanes]
        act_ref[:, lanes] = ((yn + (bf_ref[:, lanes] + bb_ref[:, lanes])) * g_ref[:, lanes]).astype(BF16)
    o_ref[...] = x_ref[...] + gate_ref[...] * _dot(act_ref[...], wo_ref[...])


def _state_to_cat(s):
    b = s.shape[0]
    return s.reshape(b, N_HG, HEADS_PER_HG, HEAD, HEAD).transpose(0, 1, 3, 2, 4).reshape(b, N_HG, HEAD, HG)


def _cat_to_state(s):
    b = s.shape[0]
    return s.reshape(b, N_HG, HEAD, HEADS_PER_HG, HEAD).transpose(0, 1, 3, 2, 4).reshape(b, N_HEAD, HEAD, HEAD)


def _rwkv_layer(x, mods, g1, s0_f, s0_b, mu, w_r, w_k, w_v, w_o, w0, w1, w2, a0, a1, a2, gl1, gl2,
                k_k, k_a, r_k, lnx_g, lnx_b):
    lora = w1.shape[-1]
    w1c = jnp.concatenate([w1[0], w1[1]], axis=1).astype(BF16)
    a1c = jnp.concatenate([a1[0], a1[1]], axis=1).astype(BF16)
    zero = jnp.zeros((lora, D), F32)
    w2c = jnp.concatenate([jnp.concatenate([w2[0], zero], 1), jnp.concatenate([zero, w2[1]], 1)], 0).astype(BF16)
    a2c = jnp.concatenate([jnp.concatenate([a2[0], zero], 1), jnp.concatenate([zero, a2[1]], 1)], 0).astype(BF16)
    gpad = MXU_DIM - gl1.shape[1]
    g1p = jnp.pad(gl1, ((0, 0), (0, gpad))).astype(BF16)
    g2p = jnp.pad(gl2, ((0, gpad), (0, 0))).astype(BF16)
    mu8 = jnp.zeros((SUBLANE, D), F32).at[:6].set(mu)
    hb = RB // SHIFT_HALO
    n_hb = N_TOK // SHIFT_HALO
    tok = jax.ShapeDtypeStruct((N_TOK, D), F32)
    r, k, v, g, kk, wl0, wl1, as0, as1 = pl.pallas_call(
        _rwkv_in_kernel,
        grid=(N_RB,),
        in_specs=[pl.BlockSpec((SHIFT_HALO, D), lambda i: (jnp.maximum(i * hb - 1, 0), 0)),
                  _row_spec(),
                  pl.BlockSpec((SHIFT_HALO, D), lambda i: (jnp.minimum((i + 1) * hb, n_hb - 1), 0)),
                  _full_spec((1, D)), _mod_spec(0), _mod_spec(1), _full_spec((SUBLANE, D)),
                  _full_spec((D, D)), _full_spec((D, D)), _full_spec((D, D)),
                  _full_spec((D, 2 * lora)), _full_spec((2 * lora, 2 * D)), _full_spec((1, 2 * D)),
                  _full_spec((D, 2 * lora)), _full_spec((2 * lora, 2 * D)), _full_spec((1, 2 * D)),
                  _full_spec((D, MXU_DIM)), _full_spec((MXU_DIM, D)), _full_spec((1, D))],
        out_specs=[_row_spec()] * 9,
        out_shape=[tok] * 9,
        scratch_shapes=[pltpu.VMEM((RB + 2 * SHIFT_HALO, D), F32)],
        compiler_params=_params(),
        name="rwkv_in",
    )(x, x, x, _row(g1), mods, mods, mu8, w_r.astype(BF16), w_k.astype(BF16), w_v.astype(BF16),
      w1c, w2c, _row(w0), a1c, a2c, _row(a0), g1p, g2p, _row(k_k))
    zeros = jnp.zeros((P_BATCH, N_HEAD, HEAD, HEAD), F32)
    s0f = _state_to_cat(jnp.concatenate([zeros, s0_f.astype(F32)], axis=0))
    s0b = _state_to_cat(jnp.concatenate([zeros, s0_b.astype(F32)], axis=0))
    ka, rk = _row(k_a), _row(r_k)
    y_f, bonus_f, sf = _wkv_scan(r, k, v, kk, wl0, as0, ka, rk, s0f, False)
    y_b, bonus_b, sb = _wkv_scan(r, k, v, kk, wl1, as1, ka, rk, s0b, True)
    x = pl.pallas_call(
        _rwkv_out_kernel,
        grid=(N_RB,),
        in_specs=[_row_spec()] * 6 + [_full_spec((1, D)), _full_spec((1, D)), _full_spec((D, D)),
                                      _mod_spec(2)],
        out_specs=_row_spec(),
        out_shape=tok,
        scratch_shapes=[pltpu.VMEM((RB, D), BF16)],
        compiler_params=_params(),
        name="rwkv_out",
    )(y_f, y_b, bonus_f, bonus_b, g, x, _row(lnx_g), _row(lnx_b), w_o.astype(BF16), mods)
    return x, _cat_to_state(sf[:P_BATCH]), _cat_to_state(sb[:P_BATCH])


KVW = N_KV * LANE
QKV_W = D + 2 * KVW


def _rope_tables():
    t = np.arange(S_SEQ)
    inv = ROPE_THETA ** (-np.arange(ROPE_PAIRS, dtype=np.float32) / ROPE_PAIRS)
    ang_r = (t // GRID_W).astype(np.float32)[:, None] * inv
    ang_c = (t % GRID_W).astype(np.float32)[:, None] * inv
    zero = np.zeros_like(ang_r)
    cos = np.concatenate([np.cos(ang_r), np.cos(ang_r), np.cos(ang_c), np.cos(ang_c)], 1)
    up = np.concatenate([-np.sin(ang_r), zero, -np.sin(ang_c), zero], 1)
    dn = np.concatenate([zero, np.sin(ang_r), zero, np.sin(ang_c)], 1)
    tile = lambda z: jnp.asarray(np.concatenate([z, z], 1), F32)
    return tile(cos), tile(up), tile(dn)


def _qkv_kernel(x_ref, g_ref, sh_ref, sc_ref, w_ref, cos_ref, up_ref, dn_ref, q_ref, k_ref, v_ref):
    h = _rms_mod(x_ref[...], g_ref[...], sh_ref[...], sc_ref[...]).astype(BF16)
    z = _dot(h, w_ref[...])
    v_ref[...] = z[:, D + KVW:]
    is_latent = pl.program_id(0) >= P_RB

    @pl.when(jnp.logical_not(is_latent))
    def _():
        q_ref[...] = z[:, :D]
        k_ref[...] = z[:, D:D + KVW]

    @pl.when(is_latent)
    def _():
        cos, up, dn = cos_ref[...], up_ref[...], dn_ref[...]
        for ct in range((D + KVW) // LANE):
            t = z[:, ct * LANE:(ct + 1) * LANE]
            rot = (t * cos + pltpu.roll(t, LANE - ROPE_PAIRS, 1) * up
                   + pltpu.roll(t, ROPE_PAIRS, 1) * dn)
            if ct < D // LANE:
                q_ref[:, ct * LANE:(ct + 1) * LANE] = rot
            else:
                k_ref[:, ct * LANE - D:(ct + 1) * LANE - D] = rot


def _head_mask(h, shape):
    left = lax.broadcasted_iota(jnp.int32, shape, 1) < HEAD
    return left if h % 2 == 0 else jnp.logical_not(left)


def _ctx_attn_kernel(q_ref, k_ref, v_ref, sink_ref, o_ref):
    for kv in range(N_KV):
        kt = k_ref[:, kv * LANE:(kv + 1) * LANE].astype(BF16)
        vt = v_ref[:, kv * LANE:(kv + 1) * LANE].astype(BF16)
        for pair in range(GROUP // 2):
            ct = kv * (GROUP // 2) + pair
            qt = q_ref[:, ct * LANE:(ct + 1) * LANE]
            outs = []
            for sub in range(2):
                h = 2 * ct + sub
                qm = jnp.where(_head_mask(h, qt.shape), qt, 0.0).astype(BF16)
                s = _dot(qm, kt, NT) * ATTN_SCALE
                sink = sink_ref[0, h]
                m = jnp.maximum(jnp.max(s, axis=-1, keepdims=True), sink)
                p = jnp.exp(s - m)
                den = jnp.sum(p, axis=-1, keepdims=True) + jnp.exp(sink - m)
                outs.append(_dot(p.astype(BF16), vt) / den)
            o_ref[:, ct * LANE:(ct + 1) * LANE] = jnp.where(_head_mask(0, qt.shape), outs[0], outs[1]).astype(BF16)


def _lat_attn_kernel(q_ref, kp_ref, kc_ref, kn_ref, vp_ref, vc_ref, vn_ref, ck_ref, cv_ref, sink_ref, o_ref):
    n = pl.program_id(1)
    row = lax.broadcasted_iota(jnp.int32, (QB, QB), 0)
    col = lax.broadcasted_iota(jnp.int32, (QB, QB), 1)
    ok_prev = (col >= row) & (n > 0)
    ok_next = (col <= row) & (n < S_SEQ // QB - 1)
    for kv in range(N_KV):
        lanes = slice(kv * LANE, (kv + 1) * LANE)
        kp, kc, kn = (ref[:, lanes].astype(BF16) for ref in (kp_ref, kc_ref, kn_ref))
        vp, vc, vn = (ref[:, lanes].astype(BF16) for ref in (vp_ref, vc_ref, vn_ref))
        ck, cv = ck_ref[:, lanes].astype(BF16), cv_ref[:, lanes].astype(BF16)
        for pair in range(GROUP // 2):
            ct = kv * (GROUP // 2) + pair
            qt = q_ref[:, ct * LANE:(ct + 1) * LANE]
            outs = []
            for sub in range(2):
                h = 2 * ct + sub
                qm = jnp.where(_head_mask(h, qt.shape), qt, 0.0).astype(BF16)
                sp = jnp.where(ok_prev, _dot(qm, kp, NT) * ATTN_SCALE, NEG_INF)
                sc = _dot(qm, kc, NT) * ATTN_SCALE
                sn = jnp.where(ok_next, _dot(qm, kn, NT) * ATTN_SCALE, NEG_INF)
                sx = _dot(qm, ck, NT) * ATTN_SCALE
                sink = sink_ref[0, h]
                m = jnp.maximum(jnp.maximum(jnp.max(sp, axis=-1, keepdims=True), jnp.max(sc, axis=-1, keepdims=True)),
                                jnp.maximum(jnp.max(sn, axis=-1, keepdims=True), jnp.max(sx, axis=-1, keepdims=True)))
                m = jnp.maximum(m, sink)
                pp, pc, pn, px = (jnp.exp(z - m) for z in (sp, sc, sn, sx))
                den = (jnp.sum(pp, axis=-1, keepdims=True) + jnp.sum(pc, axis=-1, keepdims=True)
                       + jnp.sum(pn, axis=-1, keepdims=True) + jnp.sum(px, axis=-1, keepdims=True)
                       + jnp.exp(sink - m))
                o = (_dot(pp.astype(BF16), vp) + _dot(pc.astype(BF16), vc)
                     + _dot(pn.astype(BF16), vn) + _dot(px.astype(BF16), cv))
                outs.append(o / den)
            o_ref[:, ct * LANE:(ct + 1) * LANE] = jnp.where(_head_mask(0, qt.shape), outs[0], outs[1]).astype(BF16)


def _proj_out_kernel(a_ref, x_ref, w_ref, gate_ref, o_ref):
    o_ref[...] = x_ref[...] + gate_ref[...] * _dot(a_ref[...], w_ref[...])


def _dup_heads(z):
    lead = z.shape[:-1]
    z = z.reshape(*lead, N_KV, 1, HEAD)
    return jnp.broadcast_to(z, (*lead, N_KV, 2, HEAD)).reshape(*lead, KVW)


def _attention_layer(x, mods, g1, cache_k, cache_v, w_qkv, w_o, sinks):
    nq = N_HEAD * HEAD
    nk = N_KV * HEAD
    wq, wk, wv = w_qkv[:, :nq], w_qkv[:, nq:nq + nk], w_qkv[:, nq + nk:]
    w_all = jnp.concatenate([wq, _dup_heads(wk), _dup_heads(wv)], axis=1).astype(BF16)
    cos, up, dn = _rope_tables()
    rope_spec = pl.BlockSpec((RB, LANE), lambda i: (jnp.where(i < P_RB, 0, (i - P_RB) % S_RB), 0))
    q, k, v = pl.pallas_call(
        _qkv_kernel,
        grid=(N_RB,),
        in_specs=[_row_spec(), _full_spec((1, D)), _mod_spec(0), _mod_spec(1), _full_spec((D, QKV_W)),
                  rope_spec, rope_spec, rope_spec],
        out_specs=[_row_spec(), _row_spec(KVW), _row_spec(KVW)],
        out_shape=[jax.ShapeDtypeStruct((N_TOK, D), F32), jax.ShapeDtypeStruct((N_TOK, KVW), F32),
                   jax.ShapeDtypeStruct((N_TOK, KVW), F32)],
        compiler_params=_params(),
        name="qkv",
    )(x, _row(g1), mods, mods, w_all, cos, up, dn)
    sink_row = _row(sinks.astype(F32))
    smem = pl.BlockSpec(memory_space=pltpu.SMEM)
    o_ctx = pl.pallas_call(
        _ctx_attn_kernel,
        grid=(P_BATCH,),
        in_specs=[_row_spec(D, P_SEQ), _row_spec(KVW, P_SEQ), _row_spec(KVW, P_SEQ), smem],
        out_specs=_row_spec(D, P_SEQ),
        out_shape=jax.ShapeDtypeStruct((N_PROMPT, D), BF16),
        compiler_params=_params(),
        name="ctx_attn",
    )(q, k, v, sink_row)
    nqb = S_SEQ // QB
    pq = N_PROMPT // QB

    def blk(off):
        def index(b, n):
            return (pq + b * nqb + jnp.clip(n + off, 0, nqb - 1), 0)
        return index

    ck = _dup_heads(cache_k.astype(F32).reshape(S_BATCH, PAST, nk))
    cv = _dup_heads(cache_v.astype(F32).reshape(S_BATCH, PAST, nk))
    ctx_spec = pl.BlockSpec((None, PAST, KVW), lambda b, n: (b, 0, 0))
    o_lat = pl.pallas_call(
        _lat_attn_kernel,
        grid=(S_BATCH, nqb),
        in_specs=[pl.BlockSpec((QB, D), blk(0)),
                  pl.BlockSpec((QB, KVW), blk(-1)), pl.BlockSpec((QB, KVW), blk(0)), pl.BlockSpec((QB, KVW), blk(1)),
                  pl.BlockSpec((QB, KVW), blk(-1)), pl.BlockSpec((QB, KVW), blk(0)), pl.BlockSpec((QB, KVW), blk(1)),
                  ctx_spec, ctx_spec, smem],
        out_specs=pl.BlockSpec((QB, D), lambda b, n: (b * nqb + n, 0)),
        out_shape=jax.ShapeDtypeStruct((N_SAMPLE, D), BF16),
        compiler_params=_params(2),
        name="lat_attn",
    )(q, k, k, k, v, v, v, ck, cv, sink_row)
    o = jnp.concatenate([o_ctx, o_lat], axis=0)
    x = pl.pallas_call(
        _proj_out_kernel,
        grid=(N_RB,),
        in_specs=[_row_spec(), _row_spec(), _full_spec((D, D)), _mod_spec(2)],
        out_specs=_row_spec(),
        out_shape=jax.ShapeDtypeStruct((N_TOK, D), F32),
        compiler_params=_params(),
        name="attn_out",
    )(o, x, w_o.astype(BF16), mods)
    new_k = k[:N_PROMPT].reshape(P_BATCH, P_SEQ, N_KV, 2, HEAD)[:, :, :, 0]
    new_v = v[:N_PROMPT].reshape(P_BATCH, P_SEQ, N_KV, 2, HEAD)[:, :, :, 0]
    return x, new_k, new_v


def _route_kernel(x_ref, g_ref, sh_ref, sc_ref, rw_ref, rb_ref, h_ref, idx_ref, gate_ref):
    h = _rms_mod(x_ref[...], g_ref[...], sh_ref[...], sc_ref[...])
    h_ref[...] = h
    logits = _dot3(h, rw_ref[...]) + rb_ref[...]
    lane = lax.broadcasted_iota(jnp.int32, logits.shape, 1)
    vals, idxs = [], []
    for _ in range(TOP_K):
        m = jnp.max(logits, axis=-1, keepdims=True)
        ix = jnp.min(jnp.where(logits == m, lane, LANE), axis=-1, keepdims=True)
        vals.append(m)
        idxs.append(ix)
        logits = jnp.where(lane == ix, -3e38, logits)
    es = [jnp.exp(vv - vals[0]) for vv in vals]
    den = es[0] + es[1] + es[2] + es[3]
    idx_out = jnp.zeros(logits.shape, jnp.int32)
    gate_out = jnp.zeros(logits.shape, F32)
    for kk in range(TOP_K):
        idx_out = jnp.where(lane == kk, idxs[kk], idx_out)
        gate_out = jnp.where(lane == kk, es[kk] / den, gate_out)
    idx_ref[...] = idx_out
    gate_ref[...] = gate_out


def _expert_kernel(be_ref, nu_ref, xs_ref, wgu_ref, bgu_ref, wd_ref, bd_ref, o_ref, wgu_bf, wd_bf):
    i = pl.program_id(0)

    @pl.when(i < nu_ref[0])
    def _():
        new_expert = (i == 0) | (be_ref[i] != be_ref[jnp.maximum(i - 1, 0)])

        @pl.when(new_expert)
        def _():
            wgu_bf[...] = wgu_ref[...].astype(BF16)
            wd_bf[...] = wd_ref[...].astype(BF16)

        gu = _dot(xs_ref[...], wgu_bf[...]) + bgu_ref[...]
        gate = jnp.minimum(gu[:, :D_FF], SWIGLU_LIMIT)
        up = jnp.clip(gu[:, D_FF:], -SWIGLU_LIMIT, SWIGLU_LIMIT)
        glu = gate * _sigmoid(gate * SWIGLU_ALPHA)
        act = ((up + 1.0) * glu).astype(BF16)
        o_ref[...] = _dot(act, wd_bf[...]) + bd_ref[...]

    @pl.when(i >= nu_ref[0])
    def _():
        o_ref[...] = jnp.zeros_like(o_ref)


def _combine_kernel(x_ref, yg_ref, gates_ref, gate_ref, o_ref):
    gates = gates_ref[...]
    acc = gates[:, 0:1] * yg_ref[:, 0:D]
    for kk in range(1, TOP_K):
        acc = acc + gates[:, kk:kk + 1] * yg_ref[:, kk * D:(kk + 1) * D]
    o_ref[...] = x_ref[...] + gate_ref[...] * acc


def _moe_layer(x, mods, g2, router_w, router_b, w_gu, b_gu, w_down, b_down):
    rw = jnp.zeros((D, LANE), F32).at[:, :N_EXPERTS].set(router_w)
    rb = jnp.full((1, LANE), NEG_INF, F32).at[0, :N_EXPERTS].set(router_b)
    h, idx, gates = pl.pallas_call(
        _route_kernel,
        grid=(N_RB,),
        in_specs=[_row_spec(), _full_spec((1, D)), _mod_spec(3), _mod_spec(4), _full_spec((D, LANE)),
                  _full_spec((1, LANE))],
        out_specs=[_row_spec(), _row_spec(LANE), _row_spec(LANE)],
        out_shape=[jax.ShapeDtypeStruct((N_TOK, D), F32), jax.ShapeDtypeStruct((N_TOK, LANE), jnp.int32),
                   jax.ShapeDtypeStruct((N_TOK, LANE), F32)],
        compiler_params=_params(),
        name="route",
    )(x, _row(g2), mods, mods, rw, rb)
    flat_e = idx[:, :TOP_K].reshape(N_ASSIGN)
    order = jnp.argsort(flat_e, stable=True).astype(jnp.int32)
    e_sorted = flat_e[order]
    counts = jnp.bincount(flat_e, length=N_EXPERTS).astype(jnp.int32)
    starts = jnp.cumsum(counts) - counts
    padded = (counts + TME - 1) // TME * TME
    padded_ends = jnp.cumsum(padded)
    padded_starts = padded_ends - padded
    dest_sorted = padded_starts[e_sorted] + jnp.arange(N_ASSIGN, dtype=jnp.int32) - starts[e_sorted]
    slot_tok = jnp.zeros((N_SLOTS,), jnp.int32).at[dest_sorted].set(order // TOP_K)
    dest = jnp.zeros((N_ASSIGN,), jnp.int32).at[order].set(dest_sorted)
    block_expert = jnp.minimum(
        jnp.searchsorted(padded_ends, jnp.arange(N_EBLOCKS, dtype=jnp.int32) * TME, side='right'),
        N_EXPERTS - 1).astype(jnp.int32)
    n_used = (padded_ends[-1] // TME).astype(jnp.int32).reshape(1)
    xs = h[slot_tok].astype(BF16)
    yb = pl.pallas_call(
        _expert_kernel,
        grid_spec=pltpu.PrefetchScalarGridSpec(
            num_scalar_prefetch=2,
            grid=(N_EBLOCKS,),
            in_specs=[pl.BlockSpec((TME, D), lambda i, be, nu: (i, 0)),
                      pl.BlockSpec((None, D, 2 * D_FF), lambda i, be, nu: (be[i], 0, 0)),
                      pl.BlockSpec((None, 1, 2 * D_FF), lambda i, be, nu: (be[i], 0, 0)),
                      pl.BlockSpec((None, D_FF, D), lambda i, be, nu: (be[i], 0, 0)),
                      pl.BlockSpec((None, 1, D), lambda i, be, nu: (be[i], 0, 0))],
            out_specs=pl.BlockSpec((TME, D), lambda i, be, nu: (i, 0)),
            scratch_shapes=[pltpu.VMEM((D, 2 * D_FF), BF16), pltpu.VMEM((D_FF, D), BF16)]),
        out_shape=jax.ShapeDtypeStruct((N_SLOTS, D), F32),
        compiler_params=_params(),
        name="experts",
    )(block_expert, n_used, xs, w_gu, b_gu.reshape(N_EXPERTS, 1, 2 * D_FF), w_down,
      b_down.reshape(N_EXPERTS, 1, D))
    yg = yb[dest].reshape(N_TOK, TOP_K * D)
    return pl.pallas_call(
        _combine_kernel,
        grid=(N_RB,),
        in_specs=[_row_spec(), _row_spec(TOP_K * D), _row_spec(LANE), _mod_spec(5)],
        out_specs=_row_spec(),
        out_shape=jax.ShapeDtypeStruct((N_TOK, D), F32),
        compiler_params=_params(),
        name="combine",
    )(x, yg, gates, mods)


def _final_kernel(x_ref, g_ref, o_ref):
    x = x_ref[...]
    ms = jnp.mean(x * x, axis=-1, keepdims=True)
    o_ref[...] = x * lax.rsqrt(ms + NORM_EPS) * g_ref[...]


def kernel(x_prompt, x_sample, state_wkv_fwd, state_wkv_bwd, cache_k, cache_v, c, c_ctx, ada_w, ada_b, norm1_g, norm2_g, final_g, conv_w_in, conv_b_in, conv_w_dw, conv_b_dw, conv_ln_g, conv_ln_b, conv_w_out, conv_b_out, rwkv_mu, rwkv_w_r, rwkv_w_k, rwkv_w_v, rwkv_w_o, rwkv_w0, rwkv_w1, rwkv_w2, rwkv_a0, rwkv_a1, rwkv_a2, rwkv_g1, rwkv_g2, rwkv_k_k, rwkv_k_a, rwkv_r_k, rwkv_lnx_g, rwkv_lnx_b, attn_w_qkv, attn_w_o, attn_sinks, moe_router_w, moe_router_b, moe_w_gu, moe_b_gu, moe_w_down, moe_b_down):
    x = jnp.concatenate([x_prompt.reshape(N_PROMPT, D), x_sample.reshape(N_SAMPLE, D)], axis=0).astype(F32)
    cond = jnp.zeros((N_COND, D), F32).at[0].set(c_ctx).at[1:1 + S_BATCH].set(c)
    mods_all = _ada_table(cond, ada_w, ada_b)
    new_f, new_b, new_k, new_v = [], [], [], []
    ci = ri = ai = 0
    for layer in range(DEPTH):
        mods = mods_all[layer]
        kind = layer % 3
        if kind == 0:
            x = _conformer_layer(x, mods, norm1_g[layer], conv_w_in[ci], conv_b_in[ci], conv_w_dw[ci],
                                 conv_b_dw[ci], conv_ln_g[ci], conv_ln_b[ci], conv_w_out[ci], conv_b_out[ci])
            ci += 1
        elif kind == 1:
            x, s_f, s_b = _rwkv_layer(
                x, mods, norm1_g[layer], state_wkv_fwd[:, ri], state_wkv_bwd[:, ri], rwkv_mu[ri],
                rwkv_w_r[ri], rwkv_w_k[ri], rwkv_w_v[ri], rwkv_w_o[ri], rwkv_w0[ri].reshape(-1),
                rwkv_w1[ri], rwkv_w2[ri], rwkv_a0[ri].reshape(-1), rwkv_a1[ri], rwkv_a2[ri],
                rwkv_g1[ri], rwkv_g2[ri], rwkv_k_k[ri], rwkv_k_a[ri], rwkv_r_k[ri], rwkv_lnx_g[ri],
                rwkv_lnx_b[ri])
            new_f.append(s_f.astype(x_prompt.dtype))
            new_b.append(s_b.astype(x_prompt.dtype))
            ri += 1
        else:
            x, k_ctx, v_ctx = _attention_layer(x, mods, norm1_g[layer], cache_k[:, ai], cache_v[:, ai],
                                               attn_w_qkv[ai], attn_w_o[ai], attn_sinks[ai])
            new_k.append(k_ctx)
            new_v.append(v_ctx)
            ai += 1
        x = _moe_layer(x, mods, norm2_g[layer], moe_router_w[layer], moe_router_b[layer], moe_w_gu[layer],
                       moe_b_gu[layer], moe_w_down[layer], moe_b_down[layer])
    y = pl.pallas_call(
        _final_kernel,
        grid=(N_RB,),
        in_specs=[_row_spec(), _full_spec((1, D))],
        out_specs=_row_spec(),
        out_shape=jax.ShapeDtypeStruct((N_TOK, D), F32),
        compiler_params=_params(),
        name="final_norm",
    )(x, _row(final_g))
    y_prompt = y[:N_PROMPT].reshape(P_BATCH, P_SEQ, D)
    y_sample = y[N_PROMPT:].reshape(S_BATCH, S_SEQ, D)
    return (y_prompt, y_sample, jnp.stack(new_f, axis=1), jnp.stack(new_b, axis=1),
            jnp.stack(new_k, axis=1), jnp.stack(new_v, axis=1))
```

```python
import functools

import numpy as np
import jax
import jax.numpy as jnp
from jax import lax
from jax.experimental import pallas as pl
from jax.experimental.pallas import tpu as pltpu

F32 = jnp.float32
BF16 = jnp.bfloat16

D = 1024
DEPTH = 4
P_BATCH, P_SEQ = 16, 256
S_BATCH, S_SEQ = 4, 2048
PAST = 512
GRID_W = 64
N_PROMPT = P_BATCH * P_SEQ
N_SAMPLE = S_BATCH * S_SEQ
N_TOK = N_PROMPT + N_SAMPLE
ADA_CHUNKS = 6
NORM_EPS = 1e-6
LN_EPS = 1e-5
CONV_WIDTH = 31
CONV_HALF = CONV_WIDTH // 2
HEAD = 64
N_HEAD = D // HEAD
GN_EPS = 64e-5
L2_EPS = 1e-12
N_KV = 4
GROUP = N_HEAD // N_KV
WINDOW = 128
ATTN_SCALE = HEAD ** -0.5
ROPE_THETA = 10000.0
ROPE_PAIRS = HEAD // 4
NEG_INF = -1e30
N_EXPERTS = 32
TOP_K = 4
D_FF = D
SWIGLU_LIMIT = 7.0
SWIGLU_ALPHA = 1.702

LANE = 128
SUBLANE = 8
MXU_DIM = 256
VMEM_LIMIT = 56 * 1024 * 1024

RB = 256
N_RB = N_TOK // RB
P_RB = N_PROMPT // RB
S_RB = S_SEQ // RB
N_COND = 8
CONV_HALO = 16
CONV_EXT = RB + 2 * CONV_HALO
SHIFT_HALO = SUBLANE
CHUNK = 64
N_CHUNK = N_TOK // CHUNK
HG = MXU_DIM
N_HG = D // HG
HEADS_PER_HG = HG // HEAD
N_SEQ = P_BATCH + S_BATCH
QB = 128
TME = 256
N_ASSIGN = N_TOK * TOP_K

NN = ((1,), (0,))
NT = ((1,), (1,))
TN = ((0,), (0,))


def _dot(a, b, dims=NN):
    return lax.dot_general(a, b, (dims, ((), ())), preferred_element_type=F32)


def _dot1(a, b, dims=NN):
    return _dot(a.astype(BF16), b.astype(BF16), dims)


def _split2(x):
    hi = x.astype(BF16)
    return hi, (x - hi.astype(F32)).astype(BF16)


def _dot3(a, b, dims=NN):
    ah, al = _split2(a)
    bh, bl = _split2(b)
    return _dot(ah, bh, dims) + (_dot(ah, bl, dims) + _dot(al, bh, dims))


def _sigmoid(x):
    return 1.0 / (1.0 + jnp.exp(-x))


def _rms_mod(x, g, shift, scale):
    ms = jnp.mean(x * x, axis=-1, keepdims=True)
    return (x * lax.rsqrt(ms + NORM_EPS) * g) * (1.0 + scale) + shift


def _cond_of_block(i):
    return jnp.where(i < P_RB, 0, 1 + (i - P_RB) // S_RB)


def _seq_pos(i):
    j = jnp.where(i < P_RB, 0, (i - P_RB) % S_RB)
    first = (i < P_RB) | (j == 0)
    last = (i < P_RB) | (j == S_RB - 1)
    return first, last


def _row_spec(width=D, rows=RB):
    return pl.BlockSpec((rows, width), lambda i: (i, 0))


def _mod_spec(chunk):
    return pl.BlockSpec((None, 1, D), lambda i: (_cond_of_block(i) * ADA_CHUNKS + chunk, 0, 0))


def _full_spec(shape):
    nd = len(shape)
    return pl.BlockSpec(shape, lambda i: (0,) * nd)


def _params(n_axes=1):
    return pltpu.CompilerParams(dimension_semantics=("arbitrary",) * n_axes,
                                vmem_limit_bytes=VMEM_LIMIT)


def _row(v):
    return v.reshape(1, -1)


def _ada_kernel(c_ref, w_ref, b_ref, o_ref):
    c = c_ref[...]
    o_ref[...] = _dot3(c * _sigmoid(c), w_ref[...]) + b_ref[...]


def _ada_table(cond, ada_w, ada_b):
    out = pl.pallas_call(
        _ada_kernel,
        grid=(DEPTH, ADA_CHUNKS),
        in_specs=[pl.BlockSpec((N_COND, D), lambda l, j: (0, 0)),
                  pl.BlockSpec((None, D, D), lambda l, j: (l, 0, j)),
                  pl.BlockSpec((None, 1, D), lambda l, j: (l, 0, j))],
        out_specs=pl.BlockSpec((None, N_COND, D), lambda l, j: (l, 0, j)),
        out_shape=jax.ShapeDtypeStruct((DEPTH, N_COND, ADA_CHUNKS * D), F32),
        compiler_params=_params(2),
        name="ada_table",
    )(cond, ada_w, ada_b.reshape(DEPTH, 1, ADA_CHUNKS * D))
    return out.reshape(DEPTH, N_COND * ADA_CHUNKS, 1, D)


def _conv_in_kernel(x_ref, g_ref, sh_ref, sc_ref, w_ref, b_ref, u_ref):
    h = _rms_mod(x_ref[...], g_ref[...], sh_ref[...], sc_ref[...]).astype(BF16)
    z = _dot(h, w_ref[...]) + b_ref[...]
    u_ref[...] = z[:, :D] * _sigmoid(z[:, D:])


def _conv_out_kernel(up_ref, uc_ref, un_ref, x_ref, wdw_ref, bdw_ref, lg_ref, lb_ref, wo_ref,
                     bo_ref, gate_ref, o_ref, ext_ref, act_ref):
    first, last = _seq_pos(pl.program_id(0))
    ext_ref[0, 0:CONV_HALO, :] = jnp.where(first, 0.0, up_ref[...])
    ext_ref[0, CONV_HALO:CONV_HALO + RB, :] = uc_ref[...]
    ext_ref[0, CONV_HALO + RB:, :] = jnp.where(last, 0.0, un_ref[...])
    keep = CONV_EXT - SUBLANE
    for s in range(1, SUBLANE):
        ext_ref[s, 0:keep, :] = ext_ref[0, s:s + keep, :]
    rows = 128
    base = CONV_HALO - CONV_HALF
    for ct in range(D // LANE):
        lanes = slice(ct * LANE, (ct + 1) * LANE)
        for rc in range(RB // rows):
            acc = jnp.broadcast_to(bdw_ref[:, lanes], (rows, LANE))
            for t in range(CONV_WIDTH):
                s = (base + t) % SUBLANE
                r0 = rc * rows + base + t - s
                acc = acc + ext_ref[s, r0:r0 + rows, lanes] * wdw_ref[t:t + 1, lanes]
            act_ref[rc * rows:(rc + 1) * rows, lanes] = acc
    c = act_ref[...]
    mu = jnp.mean(c, axis=-1, keepdims=True)
    cc = c - mu
    var = jnp.mean(cc * cc, axis=-1, keepdims=True)
    y = cc * lax.rsqrt(var + LN_EPS) * lg_ref[...] + lb_ref[...]
    y = y * _sigmoid(y)
    out = _dot(y.astype(BF16), wo_ref[...]) + bo_ref[...]
    o_ref[...] = x_ref[...] + gate_ref[...] * out


def _conformer_layer(x, mods, g1, w_in, b_in, w_dw, b_dw, ln_g, ln_b, w_out, b_out):
    u = pl.pallas_call(
        _conv_in_kernel,
        grid=(N_RB,),
        in_specs=[_row_spec(), _full_spec((1, D)), _mod_spec(0), _mod_spec(1),
                  _full_spec((D, 2 * D)), _full_spec((1, 2 * D))],
        out_specs=_row_spec(),
        out_shape=jax.ShapeDtypeStruct((N_TOK, D), F32),
        compiler_params=_params(),
        name="conv_in",
    )(x, _row(g1), mods, mods, w_in.astype(BF16), _row(b_in))
    hb = RB // CONV_HALO
    n_hb = N_TOK // CONV_HALO
    wdw = jnp.zeros((32, D), F32).at[:CONV_WIDTH].set(w_dw)
    return pl.pallas_call(
        _conv_out_kernel,
        grid=(N_RB,),
        in_specs=[pl.BlockSpec((CONV_HALO, D), lambda i: (jnp.maximum(i * hb - 1, 0), 0)),
                  _row_spec(),
                  pl.BlockSpec((CONV_HALO, D), lambda i: (jnp.minimum((i + 1) * hb, n_hb - 1), 0)),
                  _row_spec(), _full_spec((32, D)), _full_spec((1, D)), _full_spec((1, D)),
                  _full_spec((1, D)), _full_spec((D, D)), _full_spec((1, D)), _mod_spec(2)],
        out_specs=_row_spec(),
        out_shape=jax.ShapeDtypeStruct((N_TOK, D), F32),
        scratch_shapes=[pltpu.VMEM((SUBLANE, CONV_EXT, D), F32), pltpu.VMEM((RB, D), F32)],
        compiler_params=_params(),
        name="conv_out",
    )(u, u, u, x, wdw, _row(b_dw), _row(ln_g), _row(ln_b), w_out.astype(BF16), _row(b_out), mods)


def _softplus(z):
    return jnp.maximum(z, 0.0) + jnp.log(1.0 + jnp.exp(-jnp.abs(z)))


def _head_pair_sums(x):
    left = lax.broadcasted_iota(jnp.int32, x.shape, 1) < HEAD
    sl = jnp.sum(jnp.where(left, x, 0.0), axis=-1, keepdims=True)
    sr = jnp.sum(jnp.where(left, 0.0, x), axis=-1, keepdims=True)
    return jnp.where(left, sl, sr)


def _rwkv_in_kernel(xp_ref, xc_ref, xn_ref, g_ref, sh_ref, sc_ref, mu_ref, wr_ref, wk_ref, wv_ref,
                    w1_ref, w2_ref, w0_ref, a1_ref, a2_ref, a0_ref, g1_ref, g2_ref, kkk_ref,
                    r_o, k_o, v_o, g_o, kk_o, wl0_o, wl1_o, as0_o, as1_o, ext_ref):
    first, last = _seq_pos(pl.program_id(0))
    g, sh, sc = g_ref[...], sh_ref[...], sc_ref[...]
    h = _rms_mod(xc_ref[...], g, sh, sc)
    hp = _rms_mod(xp_ref[SHIFT_HALO - 1:SHIFT_HALO, :], g, sh, sc)
    hn = _rms_mod(xn_ref[0:1, :], g, sh, sc)
    ext_ref[SHIFT_HALO - 1:SHIFT_HALO, :] = jnp.where(first, 0.0, hp)
    ext_ref[SHIFT_HALO:SHIFT_HALO + RB, :] = h
    ext_ref[SHIFT_HALO + RB:SHIFT_HALO + RB + 1, :] = jnp.where(last, 0.0, hn)
    xx = 0.5 * (ext_ref[SHIFT_HALO - 1:SHIFT_HALO - 1 + RB, :]
                + ext_ref[SHIFT_HALO + 1:SHIFT_HALO + 1 + RB, :]) - h

    def mix(j):
        return (h + xx * mu_ref[j:j + 1, :]).astype(BF16)

    r_o[...] = _dot(mix(0), wr_ref[...])
    k = _dot(mix(2), wk_ref[...])
    k_o[...] = k
    v_o[...] = _dot(mix(3), wv_ref[...])
    t1 = jnp.tanh(_dot(mix(1), w1_ref[...]))
    wl = w0_ref[...] + _dot(t1.astype(BF16), w2_ref[...])
    wlog = -_softplus(-wl) - 0.5
    wl0_o[...] = wlog[:, :D]
    wl1_o[...] = wlog[:, D:]
    ah = _dot(mix(4), a1_ref[...])
    asig = _sigmoid(a0_ref[...] + _dot(ah.astype(BF16), a2_ref[...]))
    as0_o[...] = asig[:, :D]
    as1_o[...] = asig[:, D:]
    gh = _sigmoid(_dot(mix(5), g1_ref[...]))
    g_o[...] = _dot(gh.astype(BF16), g2_ref[...])
    kr = k * kkk_ref[...]
    for ct in range(D // LANE):
        lanes = slice(ct * LANE, (ct + 1) * LANE)
        t = kr[:, lanes]
        kk_o[:, lanes] = t * lax.rsqrt(_head_pair_sums(t * t) + L2_EPS)


def _block_diag(x, mask):
    return jnp.where(mask, jnp.concatenate([x] * HEADS_PER_HG, axis=0), 0.0)


def _wkv_groups(S, rt, at, bt, kt, v, p_end, rev):
    ids = range(len(S))
    row = lax.broadcasted_iota(jnp.int32, (CHUNK, HG), 0)
    col = lax.broadcasted_iota(jnp.int32, (CHUNK, HG), 1) % CHUNK
    strict = [row < col if rev[i] else row > col for i in ids]
    incl = [row <= col if rev[i] else row >= col for i in ids]
    bdm = (lax.broadcasted_iota(jnp.int32, (HG, HG), 0) // CHUNK
           == lax.broadcasted_iota(jnp.int32, (HG, HG), 1) // CHUNK)
    bd = functools.partial(_block_diag, mask=bdm)

    def same_block(n):
        return (row // n) == (col // n)

    def each(fn, *lists):
        return [fn(*(l[i] for l in lists)) for i in ids]

    kb, kkb, vb = each(bd, bt), each(bd, kt), each(bd, v)
    ar = each(lambda a, r: jnp.concatenate([a, r], axis=0), at, rt)
    g_b = each(lambda x, y: _dot1(x, y, NT), ar, kb)
    g_k = each(lambda x, y: _dot1(x, y, NT), ar, kkb)
    g_s = each(lambda x, y: _dot1(x, y, NT), ar, S)
    a_ab = each(lambda m, g: jnp.where(m, g[:CHUNK], 0.0), strict, g_b)
    a_rb = each(lambda m, g: jnp.where(m, g[CHUNK:], 0.0), incl, g_b)
    a_ak = each(lambda m, g: jnp.where(m, g[:CHUNK], 0.0), strict, g_k)
    a_rk = each(lambda m, g: jnp.where(m, g[CHUNK:], 0.0), incl, g_k)
    wm = each(lambda g, a, w: g[:CHUNK] + _dot1(a, w), g_s, a_ak, vb)
    n1 = each(lambda a: jnp.where(same_block(4), a, 0.0), a_ab)
    n1b = each(bd, n1)
    n2 = each(_dot1, n1, n1b)
    n3 = each(_dot1, n2, n1b)
    eye = jnp.where(row == col, 1.0, 0.0)
    t = each(lambda a, b, c: eye + a + b + c, n1, n2, n3)
    for n in (8, 16, 32):
        lower = same_block(n) & ~same_block(n // 2)
        m = each(lambda a: jnp.where(lower, a, 0.0), a_ab)
        mt = each(_dot1, m, each(bd, t))
        t = each(lambda a, b: a + _dot1(a, b), t, each(bd, mt))
    m = each(lambda a: jnp.where(same_block(32), 0.0, a), a_ab)
    x = each(_dot1, t, each(bd, wm))
    mx = each(_dot1, m, each(bd, x))
    u = each(lambda a, b, c: a + _dot1(b, c), x, t, each(bd, mx))
    ub = each(bd, u)
    y = each(lambda g, a, b, c, d: g[CHUNK:] + _dot1(a, b) + _dot1(c, d), g_s, a_rb, ub, a_rk, vb)
    uv = each(lambda a, b: jnp.concatenate([a, b], axis=0), u, v)
    bk = each(lambda a, b: jnp.concatenate([a, b], axis=0), bt, kt)
    s_new = each(lambda s, a, b, p: (s + jnp.where(bdm, _dot1(a, b, TN), 0.0)) * p, S, uv, bk, p_end)
    return s_new, y


def _wkv_prepare(r_ref, k_ref, v_ref, kk_ref, wl_ref, a_ref, ka_ref, rk_ref, bonus_ref, reverse):
    lw = -jnp.exp(wl_ref[...])
    ti = lax.broadcasted_iota(jnp.int32, (CHUNK, CHUNK), 0)
    tj = lax.broadcasted_iota(jnp.int32, (CHUNK, CHUNK), 1)
    tri = jnp.where((ti <= tj) if reverse else (ti >= tj), 1.0, 0.0).astype(BF16)
    p0 = lw.astype(BF16)
    r1 = lw - p0.astype(F32)
    p1 = r1.astype(BF16)
    p2 = (r1 - p1.astype(F32)).astype(BF16)
    li = _dot(tri, p0) + (_dot(tri, p1) + _dot(tri, p2))
    e_incl = jnp.exp(li)
    e_neg = jnp.exp(-li)
    e_excl = jnp.exp(li - lw)
    p_end = jnp.exp(jnp.sum(lw, axis=0, keepdims=True))
    r, v, kk, asig = r_ref[...], v_ref[...], kk_ref[...], a_ref[...]
    kd = k_ref[...] * (1.0 + (asig - 1.0) * ka_ref[...])
    q = r * kd * rk_ref[...]
    for ct in range(D // LANE):
        lanes = slice(ct * LANE, (ct + 1) * LANE)
        bonus_ref[:, lanes] = _head_pair_sums(q[:, lanes]) * v[:, lanes]
    return r * e_incl, -kk * e_excl, kk * asig * e_neg, kd * e_neg, v, p_end


def _wkv_kernel(seq_ref, st_ref, en_ref,
                rf_ref, kf_ref, vf_ref, kkf_ref, wlf_ref, af_ref,
                rb_ref, kb_ref, vb_ref, kkb_ref, wlb_ref, ab_ref,
                ka_ref, rk_ref, s0f_ref, s0b_ref,
                yf_ref, bonf_ref, yb_ref, bonb_ref, sff_ref, sfb_ref, sf_scr, sb_scr):
    del seq_ref
    s = pl.program_id(0)
    rs = N_CHUNK - 1 - s

    @pl.when(s == 0)
    def _():
        sf_scr[...] = jnp.zeros_like(sf_scr)
        sb_scr[...] = jnp.zeros_like(sb_scr)

    bdm = (lax.broadcasted_iota(jnp.int32, (HG, HG), 0) // CHUNK
           == lax.broadcasted_iota(jnp.int32, (HG, HG), 1) // CHUNK)
    starts = (st_ref[s] == 1, en_ref[rs] == 1)
    prep = (_wkv_prepare(rf_ref, kf_ref, vf_ref, kkf_ref, wlf_ref, af_ref, ka_ref, rk_ref, bonf_ref, False),
            _wkv_prepare(rb_ref, kb_ref, vb_ref, kkb_ref, wlb_ref, ab_ref, ka_ref, rk_ref, bonb_ref, True))
    scr, s0, y_out, s_out = (sf_scr, sb_scr), (s0f_ref, s0b_ref), (yf_ref, yb_ref), (sff_ref, sfb_ref)
    args = [[] for _ in range(7)]
    rev = []
    for d in range(2):
        for gi in range(N_HG):
            lanes = slice(gi * HG, (gi + 1) * HG)
            args[0].append(jnp.where(starts[d], _block_diag(s0[d][gi], bdm), scr[d][gi]))
            for j in range(6):
                args[j + 1].append(prep[d][j][:, lanes])
            rev.append(d == 1)
    s_new, y = _wkv_groups(*args, rev)
    for d in range(2):
        for gi in range(N_HG):
            sg = s_new[d * N_HG + gi]
            scr[d][gi] = sg
            y_out[d][:, gi * HG:(gi + 1) * HG] = y[d * N_HG + gi]
            acc = sg[0:HEAD]
            for j in range(1, HEADS_PER_HG):
                acc = acc + sg[j * HEAD:(j + 1) * HEAD]
            s_out[d][gi] = acc


def _chunk_tables():
    seq = np.concatenate([np.repeat(np.arange(P_BATCH), P_SEQ // CHUNK),
                          P_BATCH + np.repeat(np.arange(S_BATCH), S_SEQ // CHUNK)])
    start = np.ones(N_CHUNK, np.int32)
    start[1:] = seq[1:] != seq[:-1]
    end = np.ones(N_CHUNK, np.int32)
    end[:-1] = seq[1:] != seq[:-1]
    return jnp.asarray(seq, jnp.int32), jnp.asarray(start, jnp.int32), jnp.asarray(end, jnp.int32)


def _wkv_scan(r, k, v, kk, wl_f, as_f, wl_b, as_b, k_a, r_k, s0f, s0b):
    fwd = pl.BlockSpec((CHUNK, D), lambda s, seq, st, en: (s, 0))
    bwd = pl.BlockSpec((CHUNK, D), lambda s, seq, st, en: (N_CHUNK - 1 - s, 0))
    par = pl.BlockSpec((1, D), lambda s, seq, st, en: (0, 0))
    sfw = pl.BlockSpec((None, N_HG, HEAD, HG), lambda s, seq, st, en: (seq[s], 0, 0, 0))
    sbw = pl.BlockSpec((None, N_HG, HEAD, HG), lambda s, seq, st, en: (seq[N_CHUNK - 1 - s], 0, 0, 0))
    tok = jax.ShapeDtypeStruct((N_TOK, D), F32)
    state = jax.ShapeDtypeStruct((N_SEQ, N_HG, HEAD, HG), F32)
    return pl.pallas_call(
        _wkv_kernel,
        grid_spec=pltpu.PrefetchScalarGridSpec(
            num_scalar_prefetch=3,
            grid=(N_CHUNK,),
            in_specs=[fwd] * 6 + [bwd] * 6 + [par, par, sfw, sbw],
            out_specs=[fwd, fwd, bwd, bwd, sfw, sbw],
            scratch_shapes=[pltpu.VMEM((N_HG, HG, HG), F32), pltpu.VMEM((N_HG, HG, HG), F32)]),
        out_shape=[tok, tok, tok, tok, state, state],
        compiler_params=_params(),
        name="wkv",
    )(*_chunk_tables(), r, k, v, kk, wl_f, as_f, r, k, v, kk, wl_b, as_b, k_a, r_k, s0f, s0b)


def _rwkv_out_kernel(yf_ref, yb_ref, bf_ref, bb_ref, g_ref, x_ref, lg_ref, lb_ref, wo_ref, gate_ref,
                     o_ref, act_ref):
    for ct in range(D // LANE):
        lanes = slice(ct * LANE, (ct + 1) * LANE)
        y = yf_ref[:, lanes] + yb_ref[:, lanes]
        mean = _head_pair_sums(y) * (1.0 / HEAD)
        yc = y - mean
        var = _head_pair_sums(yc * yc) * (1.0 / HEAD)
        yn = yc * lax.rsqrt(var + GN_EPS) * lg_ref[:, lanes] + lb_ref[:, lanes]
        act_ref[:, lanes] = ((yn + (bf_ref[:, lanes] + bb_ref[:, lanes])) * g_ref[:, lanes]).astype(BF16)
    o_ref[...] = x_ref[...] + gate_ref[...] * _dot(act_ref[...], wo_ref[...])


def _state_to_cat(s):
    b = s.shape[0]
    return s.reshape(b, N_HG, HEADS_PER_HG, HEAD, HEAD).transpose(0, 1, 3, 2, 4).reshape(b, N_HG, HEAD, HG)


def _cat_to_state(s):
    b = s.shape[0]
    return s.reshape(b, N_HG, HEAD, HEADS_PER_HG, HEAD).transpose(0, 1, 3, 2, 4).reshape(b, N_HEAD, HEAD, HEAD)


def _rwkv_layer(x, mods, g1, s0_f, s0_b, mu, w_r, w_k, w_v, w_o, w0, w1, w2, a0, a1, a2, gl1, gl2,
                k_k, k_a, r_k, lnx_g, lnx_b):
    lora = w1.shape[-1]
    w1c = jnp.concatenate([w1[0], w1[1]], axis=1).astype(BF16)
    a1c = jnp.concatenate([a1[0], a1[1]], axis=1).astype(BF16)
    zero = jnp.zeros((lora, D), F32)
    w2c = jnp.concatenate([jnp.concatenate([w2[0], zero], 1), jnp.concatenate([zero, w2[1]], 1)], 0).astype(BF16)
    a2c = jnp.concatenate([jnp.concatenate([a2[0], zero], 1), jnp.concatenate([zero, a2[1]], 1)], 0).astype(BF16)
    gpad = MXU_DIM - gl1.shape[1]
    g1p = jnp.pad(gl1, ((0, 0), (0, gpad))).astype(BF16)
    g2p = jnp.pad(gl2, ((0, gpad), (0, 0))).astype(BF16)
    mu8 = jnp.zeros((SUBLANE, D), F32).at[:6].set(mu)
    hb = RB // SHIFT_HALO
    n_hb = N_TOK // SHIFT_HALO
    tok = jax.ShapeDtypeStruct((N_TOK, D), F32)
    r, k, v, g, kk, wl0, wl1, as0, as1 = pl.pallas_call(
        _rwkv_in_kernel,
        grid=(N_RB,),
        in_specs=[pl.BlockSpec((SHIFT_HALO, D), lambda i: (jnp.maximum(i * hb - 1, 0), 0)),
                  _row_spec(),
                  pl.BlockSpec((SHIFT_HALO, D), lambda i: (jnp.minimum((i + 1) * hb, n_hb - 1), 0)),
                  _full_spec((1, D)), _mod_spec(0), _mod_spec(1), _full_spec((SUBLANE, D)),
                  _full_spec((D, D)), _full_spec((D, D)), _full_spec((D, D)),
                  _full_spec((D, 2 * lora)), _full_spec((2 * lora, 2 * D)), _full_spec((1, 2 * D)),
                  _full_spec((D, 2 * lora)), _full_spec((2 * lora, 2 * D)), _full_spec((1, 2 * D)),
                  _full_spec((D, MXU_DIM)), _full_spec((MXU_DIM, D)), _full_spec((1, D))],
        out_specs=[_row_spec()] * 9,
        out_shape=[tok] * 9,
        scratch_shapes=[pltpu.VMEM((RB + 2 * SHIFT_HALO, D), F32)],
        compiler_params=_params(),
        name="rwkv_in",
    )(x, x, x, _row(g1), mods, mods, mu8, w_r.astype(BF16), w_k.astype(BF16), w_v.astype(BF16),
      w1c, w2c, _row(w0), a1c, a2c, _row(a0), g1p, g2p, _row(k_k))
    zeros = jnp.zeros((P_BATCH, N_HEAD, HEAD, HEAD), F32)
    s0f = _state_to_cat(jnp.concatenate([zeros, s0_f.astype(F32)], axis=0))
    s0b = _state_to_cat(jnp.concatenate([zeros, s0_b.astype(F32)], axis=0))
    ka, rk = _row(k_a), _row(r_k)
    y_f, bonus_f, y_b, bonus_b, sf, sb = _wkv_scan(r, k, v, kk, wl0, as0, wl1, as1, ka, rk, s0f, s0b)
    x = pl.pallas_call(
        _rwkv_out_kernel,
        grid=(N_RB,),
        in_specs=[_row_spec()] * 6 + [_full_spec((1, D)), _full_spec((1, D)), _full_spec((D, D)),
                                      _mod_spec(2)],
        out_specs=_row_spec(),
        out_shape=tok,
        scratch_shapes=[pltpu.VMEM((RB, D), BF16)],
        compiler_params=_params(),
        name="rwkv_out",
    )(y_f, y_b, bonus_f, bonus_b, g, x, _row(lnx_g), _row(lnx_b), w_o.astype(BF16), mods)
    return x, _cat_to_state(sf[:P_BATCH]), _cat_to_state(sb[:P_BATCH])


KVW = N_KV * LANE
QKV_W = D + 2 * KVW


def _rope_tables():
    t = np.arange(S_SEQ)
    inv = ROPE_THETA ** (-np.arange(ROPE_PAIRS, dtype=np.float32) / ROPE_PAIRS)
    ang_r = (t // GRID_W).astype(np.float32)[:, None] * inv
    ang_c = (t % GRID_W).astype(np.float32)[:, None] * inv
    zero = np.zeros_like(ang_r)
    cos = np.concatenate([np.cos(ang_r), np.cos(ang_r), np.cos(ang_c), np.cos(ang_c)], 1)
    up = np.concatenate([-np.sin(ang_r), zero, -np.sin(ang_c), zero], 1)
    dn = np.concatenate([zero, np.sin(ang_r), zero, np.sin(ang_c)], 1)
    tile = lambda z: jnp.asarray(np.concatenate([z, z], 1), F32)
    return tile(cos), tile(up), tile(dn)


def _qkv_kernel(x_ref, g_ref, sh_ref, sc_ref, w_ref, cos_ref, up_ref, dn_ref, q_ref, k_ref, v_ref):
    h = _rms_mod(x_ref[...], g_ref[...], sh_ref[...], sc_ref[...]).astype(BF16)
    z = _dot(h, w_ref[...])
    v_ref[...] = z[:, D + KVW:]
    is_latent = pl.program_id(0) >= P_RB

    @pl.when(jnp.logical_not(is_latent))
    def _():
        q_ref[...] = z[:, :D]
        k_ref[...] = z[:, D:D + KVW]

    @pl.when(is_latent)
    def _():
        cos, up, dn = cos_ref[...], up_ref[...], dn_ref[...]
        for ct in range((D + KVW) // LANE):
            t = z[:, ct * LANE:(ct + 1) * LANE]
            rot = (t * cos + pltpu.roll(t, LANE - ROPE_PAIRS, 1) * up
                   + pltpu.roll(t, ROPE_PAIRS, 1) * dn)
            if ct < D // LANE:
                q_ref[:, ct * LANE:(ct + 1) * LANE] = rot
            else:
                k_ref[:, ct * LANE - D:(ct + 1) * LANE - D] = rot


def _stack_group_queries(q_ref, kv, rows):
    left = lax.broadcasted_iota(jnp.int32, (rows, LANE), 1) < HEAD
    blocks = []
    for pair in range(GROUP // 2):
        ct = kv * (GROUP // 2) + pair
        qt = q_ref[:, ct * LANE:(ct + 1) * LANE] * ATTN_SCALE
        blocks += [jnp.where(left, qt, 0.0), jnp.where(left, 0.0, qt)]
    return jnp.concatenate(blocks, axis=0).astype(BF16)


def _group_sinks(sink_ref, kv, rows):
    head = lax.broadcasted_iota(jnp.int32, (GROUP * rows, 1), 0) // rows
    out = jnp.zeros((GROUP * rows, 1), F32)
    for j in range(GROUP):
        out = jnp.where(head == j, sink_ref[0, kv * GROUP + j], out)
    return out


def _store_group_output(o_ref, kv, o, rows):
    left = lax.broadcasted_iota(jnp.int32, (rows, LANE), 1) < HEAD
    for pair in range(GROUP // 2):
        ct = kv * (GROUP // 2) + pair
        o_ref[:, ct * LANE:(ct + 1) * LANE] = jnp.where(
            left, o[2 * pair * rows:(2 * pair + 1) * rows], o[(2 * pair + 1) * rows:(2 * pair + 2) * rows]).astype(BF16)


def _ctx_attn_kernel(q_ref, k_ref, v_ref, sink_ref, o_ref):
    kvs = range(N_KV)
    q = [_stack_group_queries(q_ref, kv, P_SEQ) for kv in kvs]
    sink = [_group_sinks(sink_ref, kv, P_SEQ) for kv in kvs]
    s = [_dot(q[kv], k_ref[:, kv * LANE:(kv + 1) * LANE].astype(BF16), NT) for kv in kvs]
    m = [jnp.maximum(jnp.max(s[kv], axis=-1, keepdims=True), sink[kv]) for kv in kvs]
    p = [jnp.exp(s[kv] - m[kv]) for kv in kvs]
    den = [jnp.sum(p[kv], axis=-1, keepdims=True) + jnp.exp(sink[kv] - m[kv]) for kv in kvs]
    o = [_dot(p[kv].astype(BF16), v_ref[:, kv * LANE:(kv + 1) * LANE].astype(BF16)) / den[kv] for kv in kvs]
    for kv in kvs:
        _store_group_output(o_ref, kv, o[kv], P_SEQ)


def _lat_attn_kernel(q_ref, kp_ref, kc_ref, kn_ref, vp_ref, vc_ref, vn_ref, ck_ref, cv_ref, sink_ref, o_ref):
    n = pl.program_id(1)
    row = lax.broadcasted_iota(jnp.int32, (GROUP * QB, QB), 0) % QB
    col = lax.broadcasted_iota(jnp.int32, (GROUP * QB, QB), 1)
    ok_prev = (col >= row) & (n > 0)
    ok_next = (col <= row) & (n < S_SEQ // QB - 1)
    kvs = range(N_KV)

    def tile(ref, kv):
        return ref[:, kv * LANE:(kv + 1) * LANE].astype(BF16)

    def rmax(z):
        return jnp.max(z, axis=-1, keepdims=True)

    def rsum(z):
        return jnp.sum(z, axis=-1, keepdims=True)

    q = [_stack_group_queries(q_ref, kv, QB) for kv in kvs]
    sink = [_group_sinks(sink_ref, kv, QB) for kv in kvs]
    sp = [jnp.where(ok_prev, _dot(q[kv], tile(kp_ref, kv), NT), NEG_INF) for kv in kvs]
    sc = [_dot(q[kv], tile(kc_ref, kv), NT) for kv in kvs]
    sn = [jnp.where(ok_next, _dot(q[kv], tile(kn_ref, kv), NT), NEG_INF) for kv in kvs]
    sx = [_dot(q[kv], tile(ck_ref, kv), NT) for kv in kvs]
    m = [jnp.maximum(jnp.maximum(jnp.maximum(rmax(sp[kv]), rmax(sc[kv])), jnp.maximum(rmax(sn[kv]), rmax(sx[kv]))),
                     sink[kv]) for kv in kvs]
    pp = [jnp.exp(sp[kv] - m[kv]) for kv in kvs]
    pc = [jnp.exp(sc[kv] - m[kv]) for kv in kvs]
    pn = [jnp.exp(sn[kv] - m[kv]) for kv in kvs]
    px = [jnp.exp(sx[kv] - m[kv]) for kv in kvs]
    den = [rsum(pp[kv]) + rsum(pc[kv]) + rsum(pn[kv]) + rsum(px[kv]) + jnp.exp(sink[kv] - m[kv]) for kv in kvs]
    o = [(_dot(pp[kv].astype(BF16), tile(vp_ref, kv)) + _dot(pc[kv].astype(BF16), tile(vc_ref, kv))
          + _dot(pn[kv].astype(BF16), tile(vn_ref, kv)) + _dot(px[kv].astype(BF16), tile(cv_ref, kv))) / den[kv]
         for kv in kvs]
    for kv in kvs:
        _store_group_output(o_ref, kv, o[kv], QB)


def _proj_out_kernel(a_ref, x_ref, w_ref, gate_ref, o_ref):
    o_ref[...] = x_ref[...] + gate_ref[...] * _dot(a_ref[...], w_ref[...])


def _dup_heads(z):
    lead = z.shape[:-1]
    z = z.reshape(*lead, N_KV, 1, HEAD)
    return jnp.broadcast_to(z, (*lead, N_KV, 2, HEAD)).reshape(*lead, KVW)


def _attention_layer(x, mods, g1, cache_k, cache_v, w_qkv, w_o, sinks):
    nq = N_HEAD * HEAD
    nk = N_KV * HEAD
    wq, wk, wv = w_qkv[:, :nq], w_qkv[:, nq:nq + nk], w_qkv[:, nq + nk:]
    w_all = jnp.concatenate([wq, _dup_heads(wk), _dup_heads(wv)], axis=1).astype(BF16)
    cos, up, dn = _rope_tables()
    rope_spec = pl.BlockSpec((RB, LANE), lambda i: (jnp.where(i < P_RB, 0, (i - P_RB) % S_RB), 0))
    q, k, v = pl.pallas_call(
        _qkv_kernel,
        grid=(N_RB,),
        in_specs=[_row_spec(), _full_spec((1, D)), _mod_spec(0), _mod_spec(1), _full_spec((D, QKV_W)),
                  rope_spec, rope_spec, rope_spec],
        out_specs=[_row_spec(), _row_spec(KVW), _row_spec(KVW)],
        out_shape=[jax.ShapeDtypeStruct((N_TOK, D), F32), jax.ShapeDtypeStruct((N_TOK, KVW), F32),
                   jax.ShapeDtypeStruct((N_TOK, KVW), F32)],
        compiler_params=_params(),
        name="qkv",
    )(x, _row(g1), mods, mods, w_all, cos, up, dn)
    sink_row = _row(sinks.astype(F32))
    smem = pl.BlockSpec(memory_space=pltpu.SMEM)
    o_ctx = pl.pallas_call(
        _ctx_attn_kernel,
        grid=(P_BATCH,),
        in_specs=[_row_spec(D, P_SEQ), _row_spec(KVW, P_SEQ), _row_spec(KVW, P_SEQ), smem],
        out_specs=_row_spec(D, P_SEQ),
        out_shape=jax.ShapeDtypeStruct((N_PROMPT, D), BF16),
        compiler_params=_params(),
        name="ctx_attn",
    )(q, k, v, sink_row)
    nqb = S_SEQ // QB
    pq = N_PROMPT // QB

    def blk(off):
        def index(b, n):
            return (pq + b * nqb + jnp.clip(n + off, 0, nqb - 1), 0)
        return index

    ck = _dup_heads(cache_k.astype(F32).reshape(S_BATCH, PAST, nk))
    cv = _dup_heads(cache_v.astype(F32).reshape(S_BATCH, PAST, nk))
    ctx_spec = pl.BlockSpec((None, PAST, KVW), lambda b, n: (b, 0, 0))
    o_lat = pl.pallas_call(
        _lat_attn_kernel,
        grid=(S_BATCH, nqb),
        in_specs=[pl.BlockSpec((QB, D), blk(0)),
                  pl.BlockSpec((QB, KVW), blk(-1)), pl.BlockSpec((QB, KVW), blk(0)), pl.BlockSpec((QB, KVW), blk(1)),
                  pl.BlockSpec((QB, KVW), blk(-1)), pl.BlockSpec((QB, KVW), blk(0)), pl.BlockSpec((QB, KVW), blk(1)),
                  ctx_spec, ctx_spec, smem],
        out_specs=pl.BlockSpec((QB, D), lambda b, n: (b * nqb + n, 0)),
        out_shape=jax.ShapeDtypeStruct((N_SAMPLE, D), BF16),
        compiler_params=_params(2),
        name="lat_attn",
    )(q, k, k, k, v, v, v, ck, cv, sink_row)
    o = jnp.concatenate([o_ctx, o_lat], axis=0)
    x = pl.pallas_call(
        _proj_out_kernel,
        grid=(N_RB,),
        in_specs=[_row_spec(), _row_spec(), _full_spec((D, D)), _mod_spec(2)],
        out_specs=_row_spec(),
        out_shape=jax.ShapeDtypeStruct((N_TOK, D), F32),
        compiler_params=_params(),
        name="attn_out",
    )(o, x, w_o.astype(BF16), mods)
    new_k = k[:N_PROMPT].reshape(P_BATCH, P_SEQ, N_KV, 2, HEAD)[:, :, :, 0]
    new_v = v[:N_PROMPT].reshape(P_BATCH, P_SEQ, N_KV, 2, HEAD)[:, :, :, 0]
    return x, new_k, new_v


RUN_ALIGN = SUBLANE
STAGE_ROWS = -(-(RB * TOP_K + N_EXPERTS * (RUN_ALIGN - 1)) // MXU_DIM) * MXU_DIM
DX = D + LANE
N_EBLOCKS = -(-(N_ASSIGN + N_RB * N_EXPERTS * (RUN_ALIGN - 1) + N_EXPERTS * (TME - 1)) // TME)
N_SLOTS = N_EBLOCKS * TME
BE_ROWS = -(-N_EBLOCKS // LANE)
ROW_RS = 0
ROW_CNT = N_RB
ROW_EEND = 2 * N_RB
ROW_PAD0 = ROW_EEND + 1
ROW_PADN = ROW_EEND + 2
ROW_USED = ROW_EEND + 3
ROW_BE = ROW_EEND + SUBLANE
PLAN_ROWS = ROW_BE + SUBLANE


def _route_kernel(x_ref, g_ref, sh_ref, sc_ref, rwt_ref, rb_ref, h_ref, idx_ref, gate_ref, cnt_ref):
    h = _rms_mod(x_ref[...], g_ref[...], sh_ref[...], sc_ref[...])
    h_ref[...] = h.astype(BF16)
    logits = _dot3(rwt_ref[...], h, NT) + rb_ref[:, 0:1]
    erow = lax.broadcasted_iota(jnp.int32, logits.shape, 0)
    vals, idxs = [], []
    for _ in range(TOP_K):
        m = jnp.max(logits, axis=0, keepdims=True)
        ix = jnp.min(jnp.where(logits == m, erow, N_EXPERTS), axis=0, keepdims=True)
        vals.append(m)
        idxs.append(ix)
        logits = jnp.where(erow == ix, -3e38, logits)
    es = [jnp.exp(vv - vals[0]) for vv in vals]
    den = es[0] + es[1] + es[2] + es[3]
    srow = lax.broadcasted_iota(jnp.int32, (SUBLANE, RB), 0)
    idx_out = jnp.full((SUBLANE, RB), -1, jnp.int32)
    gate_out = jnp.zeros((SUBLANE, RB), F32)
    osum = jnp.zeros(logits.shape, F32)
    for kk in range(TOP_K):
        idx_out = jnp.where(srow == kk, idxs[kk], idx_out)
        gate_out = jnp.where(srow == kk, es[kk] / den, gate_out)
        osum = osum + jnp.where(erow == idxs[kk], 1.0, 0.0)
    idx_ref[...] = idx_out
    gate_ref[...] = gate_out
    cnt_ref[...] = jnp.broadcast_to(jnp.sum(osum, axis=1, keepdims=True), (N_EXPERTS, LANE))


def _ceil_to(x, m):
    return jnp.floor((x + (m - 1)) * (1.0 / m)) * m


def _plan_kernel(cnt_ref, tab_ref):
    ei = lax.broadcasted_iota(jnp.int32, (N_EXPERTS, LANE), 0)
    li = lax.broadcasted_iota(jnp.int32, (N_EXPERTS, LANE), 1)
    diag = ei == li

    def to_row(col):
        return jnp.sum(jnp.where(diag, col, 0.0), axis=0, keepdims=True).astype(jnp.int32)

    total = jnp.zeros((N_EXPERTS, LANE), F32)
    for b in range(N_RB):
        total = total + _ceil_to(cnt_ref[b], RUN_ALIGN)
    padded = _ceil_to(total, TME)
    tri = (lax.broadcasted_iota(jnp.int32, (N_EXPERTS, N_EXPERTS), 0)
           >= lax.broadcasted_iota(jnp.int32, (N_EXPERTS, N_EXPERTS), 1))
    bend = _dot(jnp.where(tri, 1.0, 0.0).astype(BF16), (padded * (1.0 / TME)).astype(BF16))
    pstart = bend * TME - padded
    acc = pstart
    for b in range(N_RB):
        c = cnt_ref[b]
        tab_ref[ROW_RS + b:ROW_RS + b + 1, :] = to_row(acc)
        tab_ref[ROW_CNT + b:ROW_CNT + b + 1, :] = to_row(c)
        acc = acc + _ceil_to(c, RUN_ALIGN)
    misc = jnp.concatenate([to_row(bend), to_row(pstart + total), to_row((padded - total) * (1.0 / RUN_ALIGN)),
                            bend[N_EXPERTS - 1:N_EXPERTS, :].astype(jnp.int32),
                            jnp.zeros((SUBLANE - 4, LANE), jnp.int32)], axis=0)
    tab_ref[ROW_EEND:ROW_EEND + SUBLANE, :] = misc
    rows = []
    for part in range(BE_ROWS):
        j = (li + part * LANE).astype(F32)
        be = jnp.sum(jnp.where(bend <= j, 1.0, 0.0), axis=0, keepdims=True)
        rows.append(jnp.minimum(be, N_EXPERTS - 1.0).astype(jnp.int32))
    rows.append(jnp.zeros((SUBLANE - BE_ROWS, LANE), jnp.int32))
    tab_ref[ROW_BE:ROW_BE + SUBLANE, :] = jnp.concatenate(rows, axis=0)


def _chunks(n):
    return lax.shift_right_logical(n + (RUN_ALIGN - 1), RUN_ALIGN.bit_length() - 1)


def _block_runs(tab_ref, blk, fn):
    def per_expert(e, row):
        n_chunks = _chunks(tab_ref[(ROW_CNT + blk) * LANE + e])
        slot0 = tab_ref[(ROW_RS + blk) * LANE + e]

        def per_chunk(i, carry):
            fn(pl.multiple_of(row + i * RUN_ALIGN, RUN_ALIGN), pl.multiple_of(slot0 + i * RUN_ALIGN, RUN_ALIGN))
            return carry

        lax.fori_loop(0, n_chunks, per_chunk, 0)
        return row + n_chunks * RUN_ALIGN

    lax.fori_loop(0, N_EXPERTS, per_expert, 0)


def _dispatch_kernel(tab_ref, h_ref, idx_ref, gate_ref, xs_hbm, pos_ref, stage, zeros, sem):
    b = pl.program_id(0)
    buf = b % 2
    erow = lax.broadcasted_iota(jnp.int32, (N_EXPERTS, RB), 0)
    ots = [jnp.where(erow == idx_ref[kk:kk + 1, :], 1.0, 0.0) for kk in range(TOP_K)]
    osum = ots[0] + ots[1] + ots[2] + ots[3]
    ti = lax.broadcasted_iota(jnp.int32, (RB, RB), 0)
    tj = lax.broadcasted_iota(jnp.int32, (RB, RB), 1)
    before = _dot(osum.astype(BF16), jnp.where(ti < tj, 1.0, 0.0).astype(BF16))
    cnt8 = _ceil_to(jnp.broadcast_to(jnp.sum(osum, axis=1, keepdims=True), (N_EXPERTS, LANE)), RUN_ALIGN)
    ei = lax.broadcasted_iota(jnp.int32, (N_EXPERTS, N_EXPERTS), 0)
    ej = lax.broadcasted_iota(jnp.int32, (N_EXPERTS, N_EXPERTS), 1)
    run0 = _dot(jnp.where(ei > ej, 1.0, 0.0).astype(BF16), cnt8.astype(BF16))[:, 0:1]
    where = run0 + before
    pos = [jnp.sum(ots[kk] * where, axis=0, keepdims=True) for kk in range(TOP_K)]
    jrow = lax.broadcasted_iota(jnp.int32, (STAGE_ROWS, RB), 0).astype(F32)
    hits = [jrow == pos[kk] for kk in range(TOP_K)]
    perm = jnp.where(hits[0] | hits[1] | hits[2] | hits[3], 1.0, 0.0).astype(BF16)
    gsum = jnp.zeros((STAGE_ROWS, 1), F32)
    for kk in range(TOP_K):
        gsum = gsum + jnp.sum(jnp.where(hits[kk], gate_ref[kk:kk + 1, :], 0.0), axis=1, keepdims=True)
    stage[buf, :, 0:D] = _dot(perm, h_ref[...])
    stage[buf, :, D:DX] = jnp.broadcast_to(gsum, (STAGE_ROWS, LANE))
    srow = lax.broadcasted_iota(jnp.int32, (2 * SUBLANE, RB), 0)
    parts = jnp.zeros((2 * SUBLANE, RB), F32)
    for kk in range(TOP_K):
        hi = jnp.floor(pos[kk] * (1.0 / RUN_ALIGN))
        parts = jnp.where(srow == kk, hi, parts)
        parts = jnp.where(srow == SUBLANE + kk, pos[kk] - hi * RUN_ALIGN, parts)
    cols = _dot(jnp.where(ti == tj, 1.0, 0.0).astype(BF16), parts.astype(BF16), NT)
    posc = cols[:, 0:SUBLANE] * RUN_ALIGN + cols[:, SUBLANE:2 * SUBLANE]
    pos_ref[...] = jnp.concatenate([posc, jnp.zeros((RB, LANE - SUBLANE), F32)], axis=1).astype(jnp.int32)

    def run_copy(which):
        def make(row, slot):
            return pltpu.make_async_copy(stage.at[which, pl.ds(row, RUN_ALIGN)], xs_hbm.at[pl.ds(slot, RUN_ALIGN)],
                                         sem.at[which])
        return make

    def pad_chunks(fn):
        def per_expert(e, carry):
            slot0 = tab_ref[ROW_PAD0 * LANE + e]

            def per_chunk(i, c2):
                fn(pltpu.make_async_copy(zeros.at[pl.ds(0, RUN_ALIGN)],
                                         xs_hbm.at[pl.ds(pl.multiple_of(slot0 + i * RUN_ALIGN, RUN_ALIGN), RUN_ALIGN)],
                                         sem.at[2]))
                return c2

            lax.fori_loop(0, tab_ref[ROW_PADN * LANE + e], per_chunk, 0)
            return carry

        lax.fori_loop(0, N_EXPERTS, per_expert, 0)

        def per_block(j, carry):
            fn(pltpu.make_async_copy(zeros, xs_hbm.at[pl.ds(pl.multiple_of(j * TME, TME), TME)], sem.at[2]))
            return carry

        lax.fori_loop(tab_ref[ROW_USED * LANE], N_EBLOCKS, per_block, 0)

    @pl.when(b == 0)
    def _():
        zeros[...] = jnp.zeros_like(zeros)
        pad_chunks(lambda cp: cp.start())

    @pl.when(b > 0)
    def _():
        make = run_copy(1 - buf)
        _block_runs(tab_ref, b - 1, lambda row, slot: make(row, slot).wait())

    make = run_copy(buf)
    _block_runs(tab_ref, b, lambda row, slot: make(row, slot).start())

    @pl.when(b == N_RB - 1)
    def _():
        _block_runs(tab_ref, b, lambda row, slot: make(row, slot).wait())

    @pl.when(b == 0)
    def _():
        pad_chunks(lambda cp: cp.wait())


def _expert_kernel(tab_ref, xs_ref, bgu_ref, bd_ref, wgu_hbm, wd_hbm, o_ref,
                   wgu_f32, wd_f32, wgu_bf, wd_bf, buf_ref, sem, *, layer):
    j = pl.program_id(0)
    n_used = tab_ref[ROW_USED * LANE]

    def block_expert(jj):
        return tab_ref[ROW_BE * LANE + jj]

    def weight_copies(e, buf):
        return (pltpu.make_async_copy(wgu_hbm.at[layer, e], wgu_f32.at[buf], sem.at[0, buf]),
                pltpu.make_async_copy(wd_hbm.at[layer, e], wd_f32.at[buf], sem.at[1, buf]))

    @pl.when(j < n_used)
    def _():
        e = block_expert(j)
        new_expert = (j == 0) | (e != block_expert(jnp.maximum(j - 1, 0)))

        @pl.when(j == 0)
        def _():
            buf_ref[0] = 1
            for cp in weight_copies(e, 0):
                cp.start()

        @pl.when(new_expert)
        def _():
            buf = 1 - buf_ref[0]
            buf_ref[0] = buf
            for cp in weight_copies(e, buf):
                cp.wait()
            wgu_bf[...] = wgu_f32[buf].astype(BF16)
            wd_bf[...] = wd_f32[buf].astype(BF16)
            nxt = tab_ref[ROW_EEND * LANE + e]

            @pl.when(nxt < n_used)
            def _():
                for cp in weight_copies(block_expert(nxt), 1 - buf):
                    cp.start()

        gu = _dot(xs_ref[:, 0:D].astype(BF16), wgu_bf[...]) + bgu_ref[...]
        gate = jnp.minimum(gu[:, :D_FF], SWIGLU_LIMIT)
        up = jnp.clip(gu[:, D_FF:], -SWIGLU_LIMIT, SWIGLU_LIMIT)
        glu = gate * _sigmoid(gate * SWIGLU_ALPHA)
        act = ((up + 1.0) * glu).astype(BF16)
        o_ref[...] = (_dot(act, wd_bf[...]) + bd_ref[...]) * xs_ref[:, D:D + 1]

    @pl.when(j >= n_used)
    def _():
        o_ref[...] = jnp.zeros_like(o_ref)


def _combine_kernel(tab_ref, pos_ref, x_ref, gate_ref, fg_ref, yb_hbm, o_ref, stage, sem, *, final):
    b = pl.program_id(0)
    buf = b % 2

    def run_copy(which):
        def make(row, slot):
            return pltpu.make_async_copy(yb_hbm.at[pl.ds(slot, RUN_ALIGN)], stage.at[which, pl.ds(row, RUN_ALIGN)],
                                         sem.at[which])
        return make

    @pl.when(b == 0)
    def _():
        stage[...] = jnp.zeros_like(stage)
        make = run_copy(0)
        _block_runs(tab_ref, 0, lambda row, slot: make(row, slot).start())

    @pl.when(b + 1 < N_RB)
    def _():
        make = run_copy(1 - buf)
        _block_runs(tab_ref, b + 1, lambda row, slot: make(row, slot).start())

    lane = lax.broadcasted_iota(jnp.int32, (RB, STAGE_ROWS), 1)
    pos = pos_ref[...]
    hit = lane == pos[:, 0:1]
    for kk in range(1, TOP_K):
        hit = hit | (lane == pos[:, kk:kk + 1])
    perm = jnp.where(hit, 1.0, 0.0).astype(BF16)
    make = run_copy(buf)
    _block_runs(tab_ref, b, lambda row, slot: make(row, slot).wait())
    out = x_ref[...] + gate_ref[...] * _dot(perm, stage[buf].astype(BF16))
    if final:
        ms = jnp.mean(out * out, axis=-1, keepdims=True)
        out = out * lax.rsqrt(ms + NORM_EPS) * fg_ref[...]
    o_ref[...] = out


def _moe_layer(x, mods, g2, router_w, router_b, w_gu, b_gu, w_down, b_down, layer, final_g):
    block_tab = pl.BlockSpec((None, SUBLANE, RB), lambda i: (i, 0, 0))
    hb, idx, gates, cnt = pl.pallas_call(
        _route_kernel,
        grid=(N_RB,),
        in_specs=[_row_spec(), _full_spec((1, D)), _mod_spec(3), _mod_spec(4), _full_spec((N_EXPERTS, D)),
                  _full_spec((N_EXPERTS, LANE))],
        out_specs=[_row_spec(), block_tab, block_tab, pl.BlockSpec((None, N_EXPERTS, LANE), lambda i: (i, 0, 0))],
        out_shape=[jax.ShapeDtypeStruct((N_TOK, D), BF16), jax.ShapeDtypeStruct((N_RB, SUBLANE, RB), jnp.int32),
                   jax.ShapeDtypeStruct((N_RB, SUBLANE, RB), F32),
                   jax.ShapeDtypeStruct((N_RB, N_EXPERTS, LANE), F32)],
        compiler_params=_params(),
        name="route",
    )(x, _row(g2), mods, mods, router_w.T, jnp.broadcast_to(router_b[:, None], (N_EXPERTS, LANE)))
    tab = pl.pallas_call(
        _plan_kernel,
        out_shape=jax.ShapeDtypeStruct((PLAN_ROWS, LANE), jnp.int32),
        name="plan",
    )(cnt).reshape(PLAN_ROWS * LANE)
    any_spec = pl.BlockSpec(memory_space=pl.ANY)
    xs, pos = pl.pallas_call(
        _dispatch_kernel,
        grid_spec=pltpu.PrefetchScalarGridSpec(
            num_scalar_prefetch=1,
            grid=(N_RB,),
            in_specs=[pl.BlockSpec((RB, D), lambda i, t: (i, 0)),
                      pl.BlockSpec((None, SUBLANE, RB), lambda i, t: (i, 0, 0)),
                      pl.BlockSpec((None, SUBLANE, RB), lambda i, t: (i, 0, 0))],
            out_specs=[any_spec, pl.BlockSpec((RB, LANE), lambda i, t: (i, 0))],
            scratch_shapes=[pltpu.VMEM((2, STAGE_ROWS, DX), F32), pltpu.VMEM((TME, DX), F32),
                            pltpu.SemaphoreType.DMA((3,))]),
        out_shape=[jax.ShapeDtypeStruct((N_SLOTS, DX), F32), jax.ShapeDtypeStruct((N_TOK, LANE), jnp.int32)],
        compiler_params=_params(),
        name="dispatch",
    )(tab, hb, idx, gates)

    def used_block(j, t):
        return (jnp.minimum(j, t[ROW_USED * LANE] - 1), 0)

    def expert_row(j, t):
        return (layer, t[ROW_BE * LANE + j], 0, 0)

    yb = pl.pallas_call(
        functools.partial(_expert_kernel, layer=layer),
        grid_spec=pltpu.PrefetchScalarGridSpec(
            num_scalar_prefetch=1,
            grid=(N_EBLOCKS,),
            in_specs=[pl.BlockSpec((TME, DX), used_block),
                      pl.BlockSpec((None, None, 1, 2 * D_FF), expert_row),
                      pl.BlockSpec((None, None, 1, D), expert_row),
                      any_spec, any_spec],
            out_specs=pl.BlockSpec((TME, D), lambda j, t: (j, 0)),
            scratch_shapes=[pltpu.VMEM((2, D, 2 * D_FF), F32), pltpu.VMEM((2, D_FF, D), F32),
                            pltpu.VMEM((D, 2 * D_FF), BF16), pltpu.VMEM((D_FF, D), BF16),
                            pltpu.SMEM((1,), jnp.int32), pltpu.SemaphoreType.DMA((2, 2))]),
        out_shape=jax.ShapeDtypeStruct((N_SLOTS, D), F32),
        compiler_params=_params(),
        name="experts",
    )(tab, xs, b_gu.reshape(DEPTH, N_EXPERTS, 1, 2 * D_FF), b_down.reshape(DEPTH, N_EXPERTS, 1, D),
      w_gu, w_down)
    final = final_g is not None
    fg = _row(final_g) if final else jnp.ones((1, D), F32)
    return pl.pallas_call(
        functools.partial(_combine_kernel, final=final),
        grid_spec=pltpu.PrefetchScalarGridSpec(
            num_scalar_prefetch=1,
            grid=(N_RB,),
            in_specs=[pl.BlockSpec((RB, LANE), lambda i, t: (i, 0)), pl.BlockSpec((RB, D), lambda i, t: (i, 0)),
                      pl.BlockSpec((None, 1, D), lambda i, t: (_cond_of_block(i) * ADA_CHUNKS + 5, 0, 0)),
                      pl.BlockSpec((1, D), lambda i, t: (0, 0)), any_spec],
            out_specs=pl.BlockSpec((RB, D), lambda i, t: (i, 0)),
            scratch_shapes=[pltpu.VMEM((2, STAGE_ROWS, D), F32), pltpu.SemaphoreType.DMA((2,))]),
        out_shape=jax.ShapeDtypeStruct((N_TOK, D), F32),
        compiler_params=_params(),
        name="combine",
    )(tab, pos, x, mods, fg, yb)


def kernel(x_prompt, x_sample, state_wkv_fwd, state_wkv_bwd, cache_k, cache_v, c, c_ctx, ada_w, ada_b, norm1_g, norm2_g, final_g, conv_w_in, conv_b_in, conv_w_dw, conv_b_dw, conv_ln_g, conv_ln_b, conv_w_out, conv_b_out, rwkv_mu, rwkv_w_r, rwkv_w_k, rwkv_w_v, rwkv_w_o, rwkv_w0, rwkv_w1, rwkv_w2, rwkv_a0, rwkv_a1, rwkv_a2, rwkv_g1, rwkv_g2, rwkv_k_k, rwkv_k_a, rwkv_r_k, rwkv_lnx_g, rwkv_lnx_b, attn_w_qkv, attn_w_o, attn_sinks, moe_router_w, moe_router_b, moe_w_gu, moe_b_gu, moe_w_down, moe_b_down):
    x = jnp.concatenate([x_prompt.reshape(N_PROMPT, D), x_sample.reshape(N_SAMPLE, D)], axis=0).astype(F32)
    cond = jnp.zeros((N_COND, D), F32).at[0].set(c_ctx).at[1:1 + S_BATCH].set(c)
    mods_all = _ada_table(cond, ada_w, ada_b)
    new_f, new_b, new_k, new_v = [], [], [], []
    ci = ri = ai = 0
    for layer in range(DEPTH):
        mods = mods_all[layer]
        kind = layer % 3
        if kind == 0:
            x = _conformer_layer(x, mods, norm1_g[layer], conv_w_in[ci], conv_b_in[ci], conv_w_dw[ci],
                                 conv_b_dw[ci], conv_ln_g[ci], conv_ln_b[ci], conv_w_out[ci], conv_b_out[ci])
            ci += 1
        elif kind == 1:
            x, s_f, s_b = _rwkv_layer(
                x, mods, norm1_g[layer], state_wkv_fwd[:, ri], state_wkv_bwd[:, ri], rwkv_mu[ri],
                rwkv_w_r[ri], rwkv_w_k[ri], rwkv_w_v[ri], rwkv_w_o[ri], rwkv_w0[ri].reshape(-1),
                rwkv_w1[ri], rwkv_w2[ri], rwkv_a0[ri].reshape(-1), rwkv_a1[ri], rwkv_a2[ri],
                rwkv_g1[ri], rwkv_g2[ri], rwkv_k_k[ri], rwkv_k_a[ri], rwkv_r_k[ri], rwkv_lnx_g[ri],
                rwkv_lnx_b[ri])
            new_f.append(s_f.astype(x_prompt.dtype))
            new_b.append(s_b.astype(x_prompt.dtype))
            ri += 1
        else:
            x, k_ctx, v_ctx = _attention_layer(x, mods, norm1_g[layer], cache_k[:, ai], cache_v[:, ai],
                                               attn_w_qkv[ai], attn_w_o[ai], attn_sinks[ai])
            new_k.append(k_ctx)
            new_v.append(v_ctx)
            ai += 1
        x = _moe_layer(x, mods, norm2_g[layer], moe_router_w[layer], moe_router_b[layer], moe_w_gu, moe_b_gu,
                       moe_w_down, moe_b_down, layer, final_g if layer == DEPTH - 1 else None)
    y_prompt = x[:N_PROMPT].reshape(P_BATCH, P_SEQ, D)
    y_sample = x[N_PROMPT:].reshape(S_BATCH, S_SEQ, D)
    return (y_prompt, y_sample, jnp.stack(new_f, axis=1), jnp.stack(new_b, axis=1),
            jnp.stack(new_k, axis=1), jnp.stack(new_v, axis=1))
```

```python
import functools

import numpy as np
import jax
import jax.numpy as jnp
from jax import lax
from jax.experimental import pallas as pl
from jax.experimental.pallas import tpu as pltpu

F32 = jnp.float32
BF16 = jnp.bfloat16

D = 1024
DEPTH = 4
P_BATCH, P_SEQ = 16, 256
S_BATCH, S_SEQ = 4, 2048
PAST = 512
GRID_W = 64
N_PROMPT = P_BATCH * P_SEQ
N_SAMPLE = S_BATCH * S_SEQ
N_TOK = N_PROMPT + N_SAMPLE
ADA_CHUNKS = 6
NORM_EPS = 1e-6
LN_EPS = 1e-5
CONV_WIDTH = 31
CONV_HALF = CONV_WIDTH // 2
HEAD = 64
N_HEAD = D // HEAD
GN_EPS = 64e-5
L2_EPS = 1e-12
N_KV = 4
GROUP = N_HEAD // N_KV
WINDOW = 128
ATTN_SCALE = HEAD ** -0.5
ROPE_THETA = 10000.0
ROPE_PAIRS = HEAD // 4
NEG_INF = -1e30
N_EXPERTS = 32
TOP_K = 4
D_FF = D
SWIGLU_LIMIT = 7.0
SWIGLU_ALPHA = 1.702

LANE = 128
SUBLANE = 8
MXU_DIM = 256
VMEM_LIMIT = 56 * 1024 * 1024

RB = 256
N_RB = N_TOK // RB
P_RB = N_PROMPT // RB
S_RB = S_SEQ // RB
N_COND = 8
CONV_HALO = 16
CONV_EXT = RB + 2 * CONV_HALO
SHIFT_HALO = SUBLANE
CHUNK = 64
N_CHUNK = N_TOK // CHUNK
HG = MXU_DIM
N_HG = D // HG
HEADS_PER_HG = HG // HEAD
N_SEQ = P_BATCH + S_BATCH
QB = 128
TME = 512
FF_CHUNK = 256
N_ASSIGN = N_TOK * TOP_K

NN = ((1,), (0,))
NT = ((1,), (1,))
TN = ((0,), (0,))


def _dot(a, b, dims=NN):
    return lax.dot_general(a, b, (dims, ((), ())), preferred_element_type=F32)


def _dot1(a, b, dims=NN):
    return _dot(a.astype(BF16), b.astype(BF16), dims)


def _split2(x):
    hi = x.astype(BF16)
    return hi, (x - hi.astype(F32)).astype(BF16)


def _dot3(a, b, dims=NN):
    ah, al = _split2(a)
    bh, bl = _split2(b)
    return _dot(ah, bh, dims) + (_dot(ah, bl, dims) + _dot(al, bh, dims))


def _sigmoid(x):
    return 1.0 / (1.0 + jnp.exp(-x))


def _rms_mod(x, g, shift, scale):
    ms = jnp.mean(x * x, axis=-1, keepdims=True)
    return (x * lax.rsqrt(ms + NORM_EPS) * g) * (1.0 + scale) + shift


def _cond_of_block(i):
    return jnp.where(i < P_RB, 0, 1 + (i - P_RB) // S_RB)


def _seq_pos(i):
    j = jnp.where(i < P_RB, 0, (i - P_RB) % S_RB)
    first = (i < P_RB) | (j == 0)
    last = (i < P_RB) | (j == S_RB - 1)
    return first, last


def _row_spec(width=D, rows=RB):
    return pl.BlockSpec((rows, width), lambda i: (i, 0))


def _mod_spec(chunk):
    return pl.BlockSpec((None, 1, D), lambda i: (_cond_of_block(i) * ADA_CHUNKS + chunk, 0, 0))


def _full_spec(shape):
    nd = len(shape)
    return pl.BlockSpec(shape, lambda i: (0,) * nd)


def _params(n_axes=1):
    return pltpu.CompilerParams(dimension_semantics=("arbitrary",) * n_axes,
                                vmem_limit_bytes=VMEM_LIMIT)


def _row(v):
    return v.reshape(1, -1)


def _ada_kernel(c_ref, w_ref, b_ref, o_ref):
    c = c_ref[...]
    o_ref[...] = _dot3(c * _sigmoid(c), w_ref[...]) + b_ref[...]


def _ada_table(cond, ada_w, ada_b):
    out = pl.pallas_call(
        _ada_kernel,
        grid=(DEPTH, ADA_CHUNKS),
        in_specs=[pl.BlockSpec((N_COND, D), lambda l, j: (0, 0)),
                  pl.BlockSpec((None, D, D), lambda l, j: (l, 0, j)),
                  pl.BlockSpec((None, 1, D), lambda l, j: (l, 0, j))],
        out_specs=pl.BlockSpec((None, N_COND, D), lambda l, j: (l, 0, j)),
        out_shape=jax.ShapeDtypeStruct((DEPTH, N_COND, ADA_CHUNKS * D), F32),
        compiler_params=_params(2),
        name="ada_table",
    )(cond, ada_w, ada_b.reshape(DEPTH, 1, ADA_CHUNKS * D))
    return out.reshape(DEPTH, N_COND * ADA_CHUNKS, 1, D)


def _conv_in_kernel(x_ref, g_ref, sh_ref, sc_ref, w_ref, b_ref, u_ref):
    h = _rms_mod(x_ref[...], g_ref[...], sh_ref[...], sc_ref[...]).astype(BF16)
    z = _dot(h, w_ref[...]) + b_ref[...]
    u_ref[...] = z[:, :D] * _sigmoid(z[:, D:])


def _conv_out_kernel(up_ref, uc_ref, un_ref, x_ref, wdw_ref, bdw_ref, lg_ref, lb_ref, wo_ref,
                     bo_ref, gate_ref, o_ref, ext_ref, act_ref):
    first, last = _seq_pos(pl.program_id(0))
    ext_ref[0, 0:CONV_HALO, :] = jnp.where(first, 0.0, up_ref[...])
    ext_ref[0, CONV_HALO:CONV_HALO + RB, :] = uc_ref[...]
    ext_ref[0, CONV_HALO + RB:, :] = jnp.where(last, 0.0, un_ref[...])
    keep = CONV_EXT - SUBLANE
    for s in range(1, SUBLANE):
        ext_ref[s, 0:keep, :] = ext_ref[0, s:s + keep, :]
    rows = 128
    base = CONV_HALO - CONV_HALF
    for ct in range(D // LANE):
        lanes = slice(ct * LANE, (ct + 1) * LANE)
        for rc in range(RB // rows):
            acc = jnp.broadcast_to(bdw_ref[:, lanes], (rows, LANE))
            for t in range(CONV_WIDTH):
                s = (base + t) % SUBLANE
                r0 = rc * rows + base + t - s
                acc = acc + ext_ref[s, r0:r0 + rows, lanes] * wdw_ref[t:t + 1, lanes]
            act_ref[rc * rows:(rc + 1) * rows, lanes] = acc
    c = act_ref[...]
    mu = jnp.mean(c, axis=-1, keepdims=True)
    cc = c - mu
    var = jnp.mean(cc * cc, axis=-1, keepdims=True)
    y = cc * lax.rsqrt(var + LN_EPS) * lg_ref[...] + lb_ref[...]
    y = y * _sigmoid(y)
    out = _dot(y.astype(BF16), wo_ref[...]) + bo_ref[...]
    o_ref[...] = x_ref[...] + gate_ref[...] * out


def _conformer_layer(x, mods, g1, w_in, b_in, w_dw, b_dw, ln_g, ln_b, w_out, b_out):
    u = pl.pallas_call(
        _conv_in_kernel,
        grid=(N_RB,),
        in_specs=[_row_spec(), _full_spec((1, D)), _mod_spec(0), _mod_spec(1),
                  _full_spec((D, 2 * D)), _full_spec((1, 2 * D))],
        out_specs=_row_spec(),
        out_shape=jax.ShapeDtypeStruct((N_TOK, D), F32),
        compiler_params=_params(),
        name="conv_in",
    )(x, _row(g1), mods, mods, w_in.astype(BF16), _row(b_in))
    hb = RB // CONV_HALO
    n_hb = N_TOK // CONV_HALO
    wdw = jnp.zeros((32, D), F32).at[:CONV_WIDTH].set(w_dw)
    return pl.pallas_call(
        _conv_out_kernel,
        grid=(N_RB,),
        in_specs=[pl.BlockSpec((CONV_HALO, D), lambda i: (jnp.maximum(i * hb - 1, 0), 0)),
                  _row_spec(),
                  pl.BlockSpec((CONV_HALO, D), lambda i: (jnp.minimum((i + 1) * hb, n_hb - 1), 0)),
                  _row_spec(), _full_spec((32, D)), _full_spec((1, D)), _full_spec((1, D)),
                  _full_spec((1, D)), _full_spec((D, D)), _full_spec((1, D)), _mod_spec(2)],
        out_specs=_row_spec(),
        out_shape=jax.ShapeDtypeStruct((N_TOK, D), F32),
        scratch_shapes=[pltpu.VMEM((SUBLANE, CONV_EXT, D), F32), pltpu.VMEM((RB, D), F32)],
        compiler_params=_params(),
        name="conv_out",
    )(u, u, u, x, wdw, _row(b_dw), _row(ln_g), _row(ln_b), w_out.astype(BF16), _row(b_out), mods)


def _softplus(z):
    return jnp.maximum(z, 0.0) + jnp.log(1.0 + jnp.exp(-jnp.abs(z)))


def _head_pair_sums(x):
    left = lax.broadcasted_iota(jnp.int32, x.shape, 1) < HEAD
    sl = jnp.sum(jnp.where(left, x, 0.0), axis=-1, keepdims=True)
    sr = jnp.sum(jnp.where(left, 0.0, x), axis=-1, keepdims=True)
    return jnp.where(left, sl, sr)


def _rwkv_in_kernel(xp_ref, xc_ref, xn_ref, g_ref, sh_ref, sc_ref, mu_ref, wr_ref, wk_ref, wv_ref,
                    w1_ref, w2_ref, w0_ref, a1_ref, a2_ref, a0_ref, g1_ref, g2_ref, kkk_ref,
                    r_o, k_o, v_o, g_o, kk_o, wl0_o, wl1_o, as0_o, as1_o, ext_ref):
    first, last = _seq_pos(pl.program_id(0))
    g, sh, sc = g_ref[...], sh_ref[...], sc_ref[...]
    h = _rms_mod(xc_ref[...], g, sh, sc)
    hp = _rms_mod(xp_ref[SHIFT_HALO - 1:SHIFT_HALO, :], g, sh, sc)
    hn = _rms_mod(xn_ref[0:1, :], g, sh, sc)
    ext_ref[SHIFT_HALO - 1:SHIFT_HALO, :] = jnp.where(first, 0.0, hp)
    ext_ref[SHIFT_HALO:SHIFT_HALO + RB, :] = h
    ext_ref[SHIFT_HALO + RB:SHIFT_HALO + RB + 1, :] = jnp.where(last, 0.0, hn)
    xx = 0.5 * (ext_ref[SHIFT_HALO - 1:SHIFT_HALO - 1 + RB, :]
                + ext_ref[SHIFT_HALO + 1:SHIFT_HALO + 1 + RB, :]) - h

    def mix(j):
        return (h + xx * mu_ref[j:j + 1, :]).astype(BF16)

    r_o[...] = _dot(mix(0), wr_ref[...])
    k = _dot(mix(2), wk_ref[...])
    k_o[...] = k
    v_o[...] = _dot(mix(3), wv_ref[...])
    t1 = jnp.tanh(_dot(mix(1), w1_ref[...]))
    wl = w0_ref[...] + _dot(t1.astype(BF16), w2_ref[...])
    wlog = -_softplus(-wl) - 0.5
    wl0_o[...] = wlog[:, :D]
    wl1_o[...] = wlog[:, D:]
    ah = _dot(mix(4), a1_ref[...])
    asig = _sigmoid(a0_ref[...] + _dot(ah.astype(BF16), a2_ref[...]))
    as0_o[...] = asig[:, :D]
    as1_o[...] = asig[:, D:]
    gh = _sigmoid(_dot(mix(5), g1_ref[...]))
    g_o[...] = _dot(gh.astype(BF16), g2_ref[...])
    kr = k * kkk_ref[...]
    for ct in range(D // LANE):
        lanes = slice(ct * LANE, (ct + 1) * LANE)
        t = kr[:, lanes]
        kk_o[:, lanes] = t * lax.rsqrt(_head_pair_sums(t * t) + L2_EPS)


def _block_diag(x, mask):
    return jnp.where(mask, jnp.concatenate([x] * HEADS_PER_HG, axis=0), 0.0)


def _wkv_groups(S, rt, at, bt, kt, v, p_end, rev):
    ids = range(len(S))
    row = lax.broadcasted_iota(jnp.int32, (CHUNK, HG), 0)
    col = lax.broadcasted_iota(jnp.int32, (CHUNK, HG), 1) % CHUNK
    strict = [row < col if rev[i] else row > col for i in ids]
    incl = [row <= col if rev[i] else row >= col for i in ids]
    bdm = (lax.broadcasted_iota(jnp.int32, (HG, HG), 0) // CHUNK
           == lax.broadcasted_iota(jnp.int32, (HG, HG), 1) // CHUNK)
    bd = functools.partial(_block_diag, mask=bdm)

    def same_block(n):
        return (row // n) == (col // n)

    def each(fn, *lists):
        return [fn(*(l[i] for l in lists)) for i in ids]

    kb, kkb, vb = each(bd, bt), each(bd, kt), each(bd, v)
    ar = each(lambda a, r: jnp.concatenate([a, r], axis=0), at, rt)
    g_b = each(lambda x, y: _dot1(x, y, NT), ar, kb)
    g_k = each(lambda x, y: _dot1(x, y, NT), ar, kkb)
    g_s = each(lambda x, y: _dot1(x, y, NT), ar, S)
    a_ab = each(lambda m, g: jnp.where(m, g[:CHUNK], 0.0), strict, g_b)
    a_rb = each(lambda m, g: jnp.where(m, g[CHUNK:], 0.0), incl, g_b)
    a_ak = each(lambda m, g: jnp.where(m, g[:CHUNK], 0.0), strict, g_k)
    a_rk = each(lambda m, g: jnp.where(m, g[CHUNK:], 0.0), incl, g_k)
    wm = each(lambda g, a, w: g[:CHUNK] + _dot1(a, w), g_s, a_ak, vb)
    n1 = each(lambda a: jnp.where(same_block(4), a, 0.0), a_ab)
    n1b = each(bd, n1)
    n2 = each(_dot1, n1, n1b)
    n3 = each(_dot1, n2, n1b)
    eye = jnp.where(row == col, 1.0, 0.0)
    t = each(lambda a, b, c: eye + a + b + c, n1, n2, n3)
    for n in (8, 16, 32):
        lower = same_block(n) & ~same_block(n // 2)
        m = each(lambda a: jnp.where(lower, a, 0.0), a_ab)
        mt = each(_dot1, m, each(bd, t))
        t = each(lambda a, b: a + _dot1(a, b), t, each(bd, mt))
    m = each(lambda a: jnp.where(same_block(32), 0.0, a), a_ab)
    x = each(_dot1, t, each(bd, wm))
    mx = each(_dot1, m, each(bd, x))
    u = each(lambda a, b, c: a + _dot1(b, c), x, t, each(bd, mx))
    ub = each(bd, u)
    y = each(lambda g, a, b, c, d: g[CHUNK:] + _dot1(a, b) + _dot1(c, d), g_s, a_rb, ub, a_rk, vb)
    uv = each(lambda a, b: jnp.concatenate([a, b], axis=0), u, v)
    bk = each(lambda a, b: jnp.concatenate([a, b], axis=0), bt, kt)
    s_new = each(lambda s, a, b, p: (s + jnp.where(bdm, _dot1(a, b, TN), 0.0)) * p, S, uv, bk, p_end)
    return s_new, y


def _wkv_prepare(r_ref, k_ref, v_ref, kk_ref, wl_ref, a_ref, ka_ref, rk_ref, bonus_ref, reverse):
    lw = -jnp.exp(wl_ref[...])
    ti = lax.broadcasted_iota(jnp.int32, (CHUNK, CHUNK), 0)
    tj = lax.broadcasted_iota(jnp.int32, (CHUNK, CHUNK), 1)
    tri = jnp.where((ti <= tj) if reverse else (ti >= tj), 1.0, 0.0).astype(BF16)
    p0 = lw.astype(BF16)
    r1 = lw - p0.astype(F32)
    p1 = r1.astype(BF16)
    p2 = (r1 - p1.astype(F32)).astype(BF16)
    li = _dot(tri, p0) + (_dot(tri, p1) + _dot(tri, p2))
    e_incl = jnp.exp(li)
    e_neg = jnp.exp(-li)
    e_excl = jnp.exp(li - lw)
    p_end = jnp.exp(jnp.sum(lw, axis=0, keepdims=True))
    r, v, kk, asig = r_ref[...], v_ref[...], kk_ref[...], a_ref[...]
    kd = k_ref[...] * (1.0 + (asig - 1.0) * ka_ref[...])
    q = r * kd * rk_ref[...]
    for ct in range(D // LANE):
        lanes = slice(ct * LANE, (ct + 1) * LANE)
        bonus_ref[:, lanes] = _head_pair_sums(q[:, lanes]) * v[:, lanes]
    return r * e_incl, -kk * e_excl, kk * asig * e_neg, kd * e_neg, v, p_end


def _wkv_kernel(seq_ref, st_ref, en_ref,
                rf_ref, kf_ref, vf_ref, kkf_ref, wlf_ref, af_ref,
                rb_ref, kb_ref, vb_ref, kkb_ref, wlb_ref, ab_ref,
                ka_ref, rk_ref, s0f_ref, s0b_ref,
                yf_ref, bonf_ref, yb_ref, bonb_ref, sff_ref, sfb_ref, sf_scr, sb_scr):
    del seq_ref
    s = pl.program_id(0)
    rs = N_CHUNK - 1 - s

    @pl.when(s == 0)
    def _():
        sf_scr[...] = jnp.zeros_like(sf_scr)
        sb_scr[...] = jnp.zeros_like(sb_scr)

    bdm = (lax.broadcasted_iota(jnp.int32, (HG, HG), 0) // CHUNK
           == lax.broadcasted_iota(jnp.int32, (HG, HG), 1) // CHUNK)
    starts = (st_ref[s] == 1, en_ref[rs] == 1)
    prep = (_wkv_prepare(rf_ref, kf_ref, vf_ref, kkf_ref, wlf_ref, af_ref, ka_ref, rk_ref, bonf_ref, False),
            _wkv_prepare(rb_ref, kb_ref, vb_ref, kkb_ref, wlb_ref, ab_ref, ka_ref, rk_ref, bonb_ref, True))
    scr, s0, y_out, s_out = (sf_scr, sb_scr), (s0f_ref, s0b_ref), (yf_ref, yb_ref), (sff_ref, sfb_ref)
    args = [[] for _ in range(7)]
    rev = []
    for d in range(2):
        for gi in range(N_HG):
            lanes = slice(gi * HG, (gi + 1) * HG)
            args[0].append(jnp.where(starts[d], _block_diag(s0[d][gi], bdm), scr[d][gi]))
            for j in range(6):
                args[j + 1].append(prep[d][j][:, lanes])
            rev.append(d == 1)
    s_new, y = _wkv_groups(*args, rev)
    for d in range(2):
        for gi in range(N_HG):
            sg = s_new[d * N_HG + gi]
            scr[d][gi] = sg
            y_out[d][:, gi * HG:(gi + 1) * HG] = y[d * N_HG + gi]
            acc = sg[0:HEAD]
            for j in range(1, HEADS_PER_HG):
                acc = acc + sg[j * HEAD:(j + 1) * HEAD]
            s_out[d][gi] = acc


def _chunk_tables():
    seq = np.concatenate([np.repeat(np.arange(P_BATCH), P_SEQ // CHUNK),
                          P_BATCH + np.repeat(np.arange(S_BATCH), S_SEQ // CHUNK)])
    start = np.ones(N_CHUNK, np.int32)
    start[1:] = seq[1:] != seq[:-1]
    end = np.ones(N_CHUNK, np.int32)
    end[:-1] = seq[1:] != seq[:-1]
    return jnp.asarray(seq, jnp.int32), jnp.asarray(start, jnp.int32), jnp.asarray(end, jnp.int32)


def _wkv_scan(r, k, v, kk, wl_f, as_f, wl_b, as_b, k_a, r_k, s0f, s0b):
    fwd = pl.BlockSpec((CHUNK, D), lambda s, seq, st, en: (s, 0))
    bwd = pl.BlockSpec((CHUNK, D), lambda s, seq, st, en: (N_CHUNK - 1 - s, 0))
    par = pl.BlockSpec((1, D), lambda s, seq, st, en: (0, 0))
    sfw = pl.BlockSpec((None, N_HG, HEAD, HG), lambda s, seq, st, en: (seq[s], 0, 0, 0))
    sbw = pl.BlockSpec((None, N_HG, HEAD, HG), lambda s, seq, st, en: (seq[N_CHUNK - 1 - s], 0, 0, 0))
    tok = jax.ShapeDtypeStruct((N_TOK, D), F32)
    state = jax.ShapeDtypeStruct((N_SEQ, N_HG, HEAD, HG), F32)
    return pl.pallas_call(
        _wkv_kernel,
        grid_spec=pltpu.PrefetchScalarGridSpec(
            num_scalar_prefetch=3,
            grid=(N_CHUNK,),
            in_specs=[fwd] * 6 + [bwd] * 6 + [par, par, sfw, sbw],
            out_specs=[fwd, fwd, bwd, bwd, sfw, sbw],
            scratch_shapes=[pltpu.VMEM((N_HG, HG, HG), F32), pltpu.VMEM((N_HG, HG, HG), F32)]),
        out_shape=[tok, tok, tok, tok, state, state],
        compiler_params=_params(),
        name="wkv",
    )(*_chunk_tables(), r, k, v, kk, wl_f, as_f, r, k, v, kk, wl_b, as_b, k_a, r_k, s0f, s0b)


def _rwkv_out_kernel(yf_ref, yb_ref, bf_ref, bb_ref, g_ref, x_ref, lg_ref, lb_ref, wo_ref, gate_ref,
                     o_ref, act_ref):
    for ct in range(D // LANE):
        lanes = slice(ct * LANE, (ct + 1) * LANE)
        y = yf_ref[:, lanes] + yb_ref[:, lanes]
        mean = _head_pair_sums(y) * (1.0 / HEAD)
        yc = y - mean
        var = _head_pair_sums(yc * yc) * (1.0 / HEAD)
        yn = yc * lax.rsqrt(var + GN_EPS) * lg_ref[:, lanes] + lb_ref[:, lanes]
        act_ref[:, lanes] = ((yn + (bf_ref[:, lanes] + bb_ref[:, lanes])) * g_ref[:, lanes]).astype(BF16)
    o_ref[...] = x_ref[...] + gate_ref[...] * _dot(act_ref[...], wo_ref[...])


def _state_to_cat(s):
    b = s.shape[0]
    return s.reshape(b, N_HG, HEADS_PER_HG, HEAD, HEAD).transpose(0, 1, 3, 2, 4).reshape(b, N_HG, HEAD, HG)


def _cat_to_state(s):
    b = s.shape[0]
    return s.reshape(b, N_HG, HEAD, HEADS_PER_HG, HEAD).transpose(0, 1, 3, 2, 4).reshape(b, N_HEAD, HEAD, HEAD)


def _rwkv_layer(x, mods, g1, s0_f, s0_b, mu, w_r, w_k, w_v, w_o, w0, w1, w2, a0, a1, a2, gl1, gl2,
                k_k, k_a, r_k, lnx_g, lnx_b):
    lora = w1.shape[-1]
    w1c = jnp.concatenate([w1[0], w1[1]], axis=1).astype(BF16)
    a1c = jnp.concatenate([a1[0], a1[1]], axis=1).astype(BF16)
    zero = jnp.zeros((lora, D), F32)
    w2c = jnp.concatenate([jnp.concatenate([w2[0], zero], 1), jnp.concatenate([zero, w2[1]], 1)], 0).astype(BF16)
    a2c = jnp.concatenate([jnp.concatenate([a2[0], zero], 1), jnp.concatenate([zero, a2[1]], 1)], 0).astype(BF16)
    gpad = MXU_DIM - gl1.shape[1]
    g1p = jnp.pad(gl1, ((0, 0), (0, gpad))).astype(BF16)
    g2p = jnp.pad(gl2, ((0, gpad), (0, 0))).astype(BF16)
    mu8 = jnp.zeros((SUBLANE, D), F32).at[:6].set(mu)
    hb = RB // SHIFT_HALO
    n_hb = N_TOK // SHIFT_HALO
    tok = jax.ShapeDtypeStruct((N_TOK, D), F32)
    r, k, v, g, kk, wl0, wl1, as0, as1 = pl.pallas_call(
        _rwkv_in_kernel,
        grid=(N_RB,),
        in_specs=[pl.BlockSpec((SHIFT_HALO, D), lambda i: (jnp.maximum(i * hb - 1, 0), 0)),
                  _row_spec(),
                  pl.BlockSpec((SHIFT_HALO, D), lambda i: (jnp.minimum((i + 1) * hb, n_hb - 1), 0)),
                  _full_spec((1, D)), _mod_spec(0), _mod_spec(1), _full_spec((SUBLANE, D)),
                  _full_spec((D, D)), _full_spec((D, D)), _full_spec((D, D)),
                  _full_spec((D, 2 * lora)), _full_spec((2 * lora, 2 * D)), _full_spec((1, 2 * D)),
                  _full_spec((D, 2 * lora)), _full_spec((2 * lora, 2 * D)), _full_spec((1, 2 * D)),
                  _full_spec((D, MXU_DIM)), _full_spec((MXU_DIM, D)), _full_spec((1, D))],
        out_specs=[_row_spec()] * 9,
        out_shape=[tok] * 9,
        scratch_shapes=[pltpu.VMEM((RB + 2 * SHIFT_HALO, D), F32)],
        compiler_params=_params(),
        name="rwkv_in",
    )(x, x, x, _row(g1), mods, mods, mu8, w_r.astype(BF16), w_k.astype(BF16), w_v.astype(BF16),
      w1c, w2c, _row(w0), a1c, a2c, _row(a0), g1p, g2p, _row(k_k))
    zeros = jnp.zeros((P_BATCH, N_HEAD, HEAD, HEAD), F32)
    s0f = _state_to_cat(jnp.concatenate([zeros, s0_f.astype(F32)], axis=0))
    s0b = _state_to_cat(jnp.concatenate([zeros, s0_b.astype(F32)], axis=0))
    ka, rk = _row(k_a), _row(r_k)
    y_f, bonus_f, y_b, bonus_b, sf, sb = _wkv_scan(r, k, v, kk, wl0, as0, wl1, as1, ka, rk, s0f, s0b)
    x = pl.pallas_call(
        _rwkv_out_kernel,
        grid=(N_RB,),
        in_specs=[_row_spec()] * 6 + [_full_spec((1, D)), _full_spec((1, D)), _full_spec((D, D)),
                                      _mod_spec(2)],
        out_specs=_row_spec(),
        out_shape=tok,
        scratch_shapes=[pltpu.VMEM((RB, D), BF16)],
        compiler_params=_params(),
        name="rwkv_out",
    )(y_f, y_b, bonus_f, bonus_b, g, x, _row(lnx_g), _row(lnx_b), w_o.astype(BF16), mods)
    return x, _cat_to_state(sf[:P_BATCH]), _cat_to_state(sb[:P_BATCH])


KVW = N_KV * LANE
QKV_W = D + 2 * KVW


def _rope_tables():
    t = np.arange(S_SEQ)
    inv = ROPE_THETA ** (-np.arange(ROPE_PAIRS, dtype=np.float32) / ROPE_PAIRS)
    ang_r = (t // GRID_W).astype(np.float32)[:, None] * inv
    ang_c = (t % GRID_W).astype(np.float32)[:, None] * inv
    zero = np.zeros_like(ang_r)
    cos = np.concatenate([np.cos(ang_r), np.cos(ang_r), np.cos(ang_c), np.cos(ang_c)], 1)
    up = np.concatenate([-np.sin(ang_r), zero, -np.sin(ang_c), zero], 1)
    dn = np.concatenate([zero, np.sin(ang_r), zero, np.sin(ang_c)], 1)
    tile = lambda z: jnp.asarray(np.concatenate([z, z], 1), F32)
    return tile(cos), tile(up), tile(dn)


def _qkv_kernel(x_ref, g_ref, sh_ref, sc_ref, w_ref, cos_ref, up_ref, dn_ref, q_ref, k_ref, v_ref):
    h = _rms_mod(x_ref[...], g_ref[...], sh_ref[...], sc_ref[...]).astype(BF16)
    z = _dot(h, w_ref[...])
    v_ref[...] = z[:, D + KVW:]
    is_latent = pl.program_id(0) >= P_RB

    @pl.when(jnp.logical_not(is_latent))
    def _():
        q_ref[...] = z[:, :D]
        k_ref[...] = z[:, D:D + KVW]

    @pl.when(is_latent)
    def _():
        cos, up, dn = cos_ref[...], up_ref[...], dn_ref[...]
        for ct in range((D + KVW) // LANE):
            t = z[:, ct * LANE:(ct + 1) * LANE]
            rot = (t * cos + pltpu.roll(t, LANE - ROPE_PAIRS, 1) * up
                   + pltpu.roll(t, ROPE_PAIRS, 1) * dn)
            if ct < D // LANE:
                q_ref[:, ct * LANE:(ct + 1) * LANE] = rot
            else:
                k_ref[:, ct * LANE - D:(ct + 1) * LANE - D] = rot


def _stack_group_queries(q_ref, kv, rows):
    left = lax.broadcasted_iota(jnp.int32, (rows, LANE), 1) < HEAD
    blocks = []
    for pair in range(GROUP // 2):
        ct = kv * (GROUP // 2) + pair
        qt = q_ref[:, ct * LANE:(ct + 1) * LANE] * ATTN_SCALE
        blocks += [jnp.where(left, qt, 0.0), jnp.where(left, 0.0, qt)]
    return jnp.concatenate(blocks, axis=0).astype(BF16)


def _group_sinks(sink_ref, kv, rows):
    head = lax.broadcasted_iota(jnp.int32, (GROUP * rows, 1), 0) // rows
    out = jnp.zeros((GROUP * rows, 1), F32)
    for j in range(GROUP):
        out = jnp.where(head == j, sink_ref[0, kv * GROUP + j], out)
    return out


def _store_group_output(o_ref, kv, o, rows):
    left = lax.broadcasted_iota(jnp.int32, (rows, LANE), 1) < HEAD
    for pair in range(GROUP // 2):
        ct = kv * (GROUP // 2) + pair
        o_ref[:, ct * LANE:(ct + 1) * LANE] = jnp.where(
            left, o[2 * pair * rows:(2 * pair + 1) * rows], o[(2 * pair + 1) * rows:(2 * pair + 2) * rows]).astype(BF16)


def _ctx_attn_kernel(q_ref, k_ref, v_ref, sink_ref, o_ref):
    kvs = range(N_KV)
    q = [_stack_group_queries(q_ref, kv, P_SEQ) for kv in kvs]
    sink = [_group_sinks(sink_ref, kv, P_SEQ) for kv in kvs]
    s = [_dot(q[kv], k_ref[:, kv * LANE:(kv + 1) * LANE].astype(BF16), NT) for kv in kvs]
    m = [jnp.maximum(jnp.max(s[kv], axis=-1, keepdims=True), sink[kv]) for kv in kvs]
    p = [jnp.exp(s[kv] - m[kv]) for kv in kvs]
    den = [jnp.sum(p[kv], axis=-1, keepdims=True) + jnp.exp(sink[kv] - m[kv]) for kv in kvs]
    o = [_dot(p[kv].astype(BF16), v_ref[:, kv * LANE:(kv + 1) * LANE].astype(BF16)) / den[kv] for kv in kvs]
    for kv in kvs:
        _store_group_output(o_ref, kv, o[kv], P_SEQ)


def _lat_attn_kernel(q_ref, kp_ref, kc_ref, kn_ref, vp_ref, vc_ref, vn_ref, ck_ref, cv_ref, sink_ref, o_ref):
    n = pl.program_id(1)
    row = lax.broadcasted_iota(jnp.int32, (GROUP * QB, QB), 0) % QB
    col = lax.broadcasted_iota(jnp.int32, (GROUP * QB, QB), 1)
    ok_prev = (col >= row) & (n > 0)
    ok_next = (col <= row) & (n < S_SEQ // QB - 1)
    kvs = range(N_KV)

    def tile(ref, kv):
        return ref[:, kv * LANE:(kv + 1) * LANE].astype(BF16)

    def rmax(z):
        return jnp.max(z, axis=-1, keepdims=True)

    def rsum(z):
        return jnp.sum(z, axis=-1, keepdims=True)

    q = [_stack_group_queries(q_ref, kv, QB) for kv in kvs]
    sink = [_group_sinks(sink_ref, kv, QB) for kv in kvs]
    sp = [jnp.where(ok_prev, _dot(q[kv], tile(kp_ref, kv), NT), NEG_INF) for kv in kvs]
    sc = [_dot(q[kv], tile(kc_ref, kv), NT) for kv in kvs]
    sn = [jnp.where(ok_next, _dot(q[kv], tile(kn_ref, kv), NT), NEG_INF) for kv in kvs]
    sx = [_dot(q[kv], tile(ck_ref, kv), NT) for kv in kvs]
    m = [jnp.maximum(jnp.maximum(jnp.maximum(rmax(sp[kv]), rmax(sc[kv])), jnp.maximum(rmax(sn[kv]), rmax(sx[kv]))),
                     sink[kv]) for kv in kvs]
    pp = [jnp.exp(sp[kv] - m[kv]) for kv in kvs]
    pc = [jnp.exp(sc[kv] - m[kv]) for kv in kvs]
    pn = [jnp.exp(sn[kv] - m[kv]) for kv in kvs]
    px = [jnp.exp(sx[kv] - m[kv]) for kv in kvs]
    den = [rsum(pp[kv]) + rsum(pc[kv]) + rsum(pn[kv]) + rsum(px[kv]) + jnp.exp(sink[kv] - m[kv]) for kv in kvs]
    o = [(_dot(pp[kv].astype(BF16), tile(vp_ref, kv)) + _dot(pc[kv].astype(BF16), tile(vc_ref, kv))
          + _dot(pn[kv].astype(BF16), tile(vn_ref, kv)) + _dot(px[kv].astype(BF16), tile(cv_ref, kv))) / den[kv]
         for kv in kvs]
    for kv in kvs:
        _store_group_output(o_ref, kv, o[kv], QB)


def _proj_out_kernel(a_ref, x_ref, w_ref, gate_ref, o_ref):
    o_ref[...] = x_ref[...] + gate_ref[...] * _dot(a_ref[...], w_ref[...])


def _dup_heads(z):
    lead = z.shape[:-1]
    z = z.reshape(*lead, N_KV, 1, HEAD)
    return jnp.broadcast_to(z, (*lead, N_KV, 2, HEAD)).reshape(*lead, KVW)


def _attention_layer(x, mods, g1, cache_k, cache_v, w_qkv, w_o, sinks):
    nq = N_HEAD * HEAD
    nk = N_KV * HEAD
    wq, wk, wv = w_qkv[:, :nq], w_qkv[:, nq:nq + nk], w_qkv[:, nq + nk:]
    w_all = jnp.concatenate([wq, _dup_heads(wk), _dup_heads(wv)], axis=1).astype(BF16)
    cos, up, dn = _rope_tables()
    rope_spec = pl.BlockSpec((RB, LANE), lambda i: (jnp.where(i < P_RB, 0, (i - P_RB) % S_RB), 0))
    q, k, v = pl.pallas_call(
        _qkv_kernel,
        grid=(N_RB,),
        in_specs=[_row_spec(), _full_spec((1, D)), _mod_spec(0), _mod_spec(1), _full_spec((D, QKV_W)),
                  rope_spec, rope_spec, rope_spec],
        out_specs=[_row_spec(), _row_spec(KVW), _row_spec(KVW)],
        out_shape=[jax.ShapeDtypeStruct((N_TOK, D), F32), jax.ShapeDtypeStruct((N_TOK, KVW), F32),
                   jax.ShapeDtypeStruct((N_TOK, KVW), F32)],
        compiler_params=_params(),
        name="qkv",
    )(x, _row(g1), mods, mods, w_all, cos, up, dn)
    sink_row = _row(sinks.astype(F32))
    smem = pl.BlockSpec(memory_space=pltpu.SMEM)
    o_ctx = pl.pallas_call(
        _ctx_attn_kernel,
        grid=(P_BATCH,),
        in_specs=[_row_spec(D, P_SEQ), _row_spec(KVW, P_SEQ), _row_spec(KVW, P_SEQ), smem],
        out_specs=_row_spec(D, P_SEQ),
        out_shape=jax.ShapeDtypeStruct((N_PROMPT, D), BF16),
        compiler_params=_params(),
        name="ctx_attn",
    )(q, k, v, sink_row)
    nqb = S_SEQ // QB
    pq = N_PROMPT // QB

    def blk(off):
        def index(b, n):
            return (pq + b * nqb + jnp.clip(n + off, 0, nqb - 1), 0)
        return index

    ck = _dup_heads(cache_k.astype(F32).reshape(S_BATCH, PAST, nk))
    cv = _dup_heads(cache_v.astype(F32).reshape(S_BATCH, PAST, nk))
    ctx_spec = pl.BlockSpec((None, PAST, KVW), lambda b, n: (b, 0, 0))
    o_lat = pl.pallas_call(
        _lat_attn_kernel,
        grid=(S_BATCH, nqb),
        in_specs=[pl.BlockSpec((QB, D), blk(0)),
                  pl.BlockSpec((QB, KVW), blk(-1)), pl.BlockSpec((QB, KVW), blk(0)), pl.BlockSpec((QB, KVW), blk(1)),
                  pl.BlockSpec((QB, KVW), blk(-1)), pl.BlockSpec((QB, KVW), blk(0)), pl.BlockSpec((QB, KVW), blk(1)),
                  ctx_spec, ctx_spec, smem],
        out_specs=pl.BlockSpec((QB, D), lambda b, n: (b * nqb + n, 0)),
        out_shape=jax.ShapeDtypeStruct((N_SAMPLE, D), BF16),
        compiler_params=_params(2),
        name="lat_attn",
    )(q, k, k, k, v, v, v, ck, cv, sink_row)
    o = jnp.concatenate([o_ctx, o_lat], axis=0)
    x = pl.pallas_call(
        _proj_out_kernel,
        grid=(N_RB,),
        in_specs=[_row_spec(), _row_spec(), _full_spec((D, D)), _mod_spec(2)],
        out_specs=_row_spec(),
        out_shape=jax.ShapeDtypeStruct((N_TOK, D), F32),
        compiler_params=_params(),
        name="attn_out",
    )(o, x, w_o.astype(BF16), mods)
    new_k = k[:N_PROMPT].reshape(P_BATCH, P_SEQ, N_KV, 2, HEAD)[:, :, :, 0]
    new_v = v[:N_PROMPT].reshape(P_BATCH, P_SEQ, N_KV, 2, HEAD)[:, :, :, 0]
    return x, new_k, new_v


RUN_ALIGN = 2 * SUBLANE
STAGE_ROWS = -(-(RB * TOP_K + N_EXPERTS * (RUN_ALIGN - 1)) // MXU_DIM) * MXU_DIM
DX = D + 2 * LANE
N_EBLOCKS = -(-(N_ASSIGN + N_RB * N_EXPERTS * (RUN_ALIGN - 1) + N_EXPERTS * (TME - 1)) // TME)
N_SLOTS = N_EBLOCKS * TME
BE_ROWS = -(-N_EBLOCKS // LANE)
ROW_RS = 0
ROW_CNT = N_RB
ROW_EEND = 2 * N_RB
ROW_PAD0 = ROW_EEND + 1
ROW_PADN = ROW_EEND + 2
ROW_USED = ROW_EEND + 3
ROW_BE = ROW_EEND + SUBLANE
PLAN_ROWS = ROW_BE + SUBLANE


def _route_kernel(x_ref, g_ref, sh_ref, sc_ref, rwt_ref, rb_ref, h_ref, idx_ref, gate_ref, cnt_ref):
    h = _rms_mod(x_ref[...], g_ref[...], sh_ref[...], sc_ref[...])
    h_ref[...] = h.astype(BF16)
    logits = _dot3(rwt_ref[...], h, NT) + rb_ref[:, 0:1]
    erow = lax.broadcasted_iota(jnp.int32, logits.shape, 0)
    vals, idxs = [], []
    for _ in range(TOP_K):
        m = jnp.max(logits, axis=0, keepdims=True)
        ix = jnp.min(jnp.where(logits == m, erow, N_EXPERTS), axis=0, keepdims=True)
        vals.append(m)
        idxs.append(ix)
        logits = jnp.where(erow == ix, -3e38, logits)
    es = [jnp.exp(vv - vals[0]) for vv in vals]
    den = es[0] + es[1] + es[2] + es[3]
    srow = lax.broadcasted_iota(jnp.int32, (SUBLANE, RB), 0)
    idx_out = jnp.full((SUBLANE, RB), -1, jnp.int32)
    gate_out = jnp.zeros((SUBLANE, RB), F32)
    osum = jnp.zeros(logits.shape, F32)
    for kk in range(TOP_K):
        idx_out = jnp.where(srow == kk, idxs[kk], idx_out)
        gate_out = jnp.where(srow == kk, es[kk] / den, gate_out)
        osum = osum + jnp.where(erow == idxs[kk], 1.0, 0.0)
    idx_ref[...] = idx_out
    gate_ref[...] = gate_out
    cnt_ref[...] = jnp.broadcast_to(jnp.sum(osum, axis=1, keepdims=True), (N_EXPERTS, LANE))


def _ceil_to(x, m):
    return jnp.floor((x + (m - 1)) * (1.0 / m)) * m


def _plan_kernel(cnt_ref, tab_ref):
    ei = lax.broadcasted_iota(jnp.int32, (N_EXPERTS, LANE), 0)
    li = lax.broadcasted_iota(jnp.int32, (N_EXPERTS, LANE), 1)
    diag = ei == li

    def to_row(col):
        return jnp.sum(jnp.where(diag, col, 0.0), axis=0, keepdims=True).astype(jnp.int32)

    total = jnp.zeros((N_EXPERTS, LANE), F32)
    for b in range(N_RB):
        total = total + _ceil_to(cnt_ref[b], RUN_ALIGN)
    padded = _ceil_to(total, TME)
    tri = (lax.broadcasted_iota(jnp.int32, (N_EXPERTS, N_EXPERTS), 0)
           >= lax.broadcasted_iota(jnp.int32, (N_EXPERTS, N_EXPERTS), 1))
    bend = _dot(jnp.where(tri, 1.0, 0.0).astype(BF16), (padded * (1.0 / TME)).astype(BF16))
    pstart = bend * TME - padded
    acc = pstart
    for b in range(N_RB):
        c = cnt_ref[b]
        tab_ref[ROW_RS + b:ROW_RS + b + 1, :] = to_row(acc)
        tab_ref[ROW_CNT + b:ROW_CNT + b + 1, :] = to_row(c)
        acc = acc + _ceil_to(c, RUN_ALIGN)
    misc = jnp.concatenate([to_row(bend), to_row(pstart + total), to_row((padded - total) * (1.0 / RUN_ALIGN)),
                            bend[N_EXPERTS - 1:N_EXPERTS, :].astype(jnp.int32),
                            jnp.zeros((SUBLANE - 4, LANE), jnp.int32)], axis=0)
    tab_ref[ROW_EEND:ROW_EEND + SUBLANE, :] = misc
    rows = []
    for part in range(BE_ROWS):
        j = (li + part * LANE).astype(F32)
        be = jnp.sum(jnp.where(bend <= j, 1.0, 0.0), axis=0, keepdims=True)
        rows.append(jnp.minimum(be, N_EXPERTS - 1.0).astype(jnp.int32))
    rows.append(jnp.zeros((SUBLANE - BE_ROWS, LANE), jnp.int32))
    tab_ref[ROW_BE:ROW_BE + SUBLANE, :] = jnp.concatenate(rows, axis=0)


def _chunks(n):
    return lax.shift_right_logical(n + (RUN_ALIGN - 1), RUN_ALIGN.bit_length() - 1)


def _block_runs(tab_ref, blk, fn):
    def per_expert(e, row):
        rows = pl.multiple_of(_chunks(tab_ref[(ROW_CNT + blk) * LANE + e]) * RUN_ALIGN, RUN_ALIGN)

        @pl.when(rows > 0)
        def _():
            fn(pl.multiple_of(row, RUN_ALIGN), pl.multiple_of(tab_ref[(ROW_RS + blk) * LANE + e], RUN_ALIGN), rows)

        return row + rows

    lax.fori_loop(0, N_EXPERTS, per_expert, 0)


def _dispatch_kernel(tab_ref, h_ref, idx_ref, gate_ref, xs_hbm, pos_ref, stage, zeros, sem):
    b = pl.program_id(0)
    buf = b % 2
    erow = lax.broadcasted_iota(jnp.int32, (N_EXPERTS, RB), 0)
    ots = [jnp.where(erow == idx_ref[kk:kk + 1, :], 1.0, 0.0) for kk in range(TOP_K)]
    osum = ots[0] + ots[1] + ots[2] + ots[3]
    ti = lax.broadcasted_iota(jnp.int32, (RB, RB), 0)
    tj = lax.broadcasted_iota(jnp.int32, (RB, RB), 1)
    before = _dot(osum.astype(BF16), jnp.where(ti < tj, 1.0, 0.0).astype(BF16))
    cnt8 = _ceil_to(jnp.broadcast_to(jnp.sum(osum, axis=1, keepdims=True), (N_EXPERTS, LANE)), RUN_ALIGN)
    ei = lax.broadcasted_iota(jnp.int32, (N_EXPERTS, N_EXPERTS), 0)
    ej = lax.broadcasted_iota(jnp.int32, (N_EXPERTS, N_EXPERTS), 1)
    run0 = _dot(jnp.where(ei > ej, 1.0, 0.0).astype(BF16), cnt8.astype(BF16))[:, 0:1]
    where = run0 + before
    pos = [jnp.sum(ots[kk] * where, axis=0, keepdims=True) for kk in range(TOP_K)]
    jrow = lax.broadcasted_iota(jnp.int32, (STAGE_ROWS, RB), 0).astype(F32)
    hits = [jrow == pos[kk] for kk in range(TOP_K)]
    perm = jnp.where(hits[0] | hits[1] | hits[2] | hits[3], 1.0, 0.0).astype(BF16)
    gsum = jnp.zeros((STAGE_ROWS, 1), F32)
    for kk in range(TOP_K):
        gsum = gsum + jnp.sum(jnp.where(hits[kk], gate_ref[kk:kk + 1, :], 0.0), axis=1, keepdims=True)
    stage[buf, :, 0:D] = _dot(perm, h_ref[...]).astype(BF16)
    g_hi = gsum.astype(BF16)
    g_lo = (gsum - g_hi.astype(F32)).astype(BF16)
    stage[buf, :, D:D + LANE] = jnp.broadcast_to(g_hi, (STAGE_ROWS, LANE))
    stage[buf, :, D + LANE:DX] = jnp.broadcast_to(g_lo, (STAGE_ROWS, LANE))
    srow = lax.broadcasted_iota(jnp.int32, (2 * SUBLANE, RB), 0)
    parts = jnp.zeros((2 * SUBLANE, RB), F32)
    for kk in range(TOP_K):
        hi = jnp.floor(pos[kk] * (1.0 / RUN_ALIGN))
        parts = jnp.where(srow == kk, hi, parts)
        parts = jnp.where(srow == SUBLANE + kk, pos[kk] - hi * RUN_ALIGN, parts)
    cols = _dot(jnp.where(ti == tj, 1.0, 0.0).astype(BF16), parts.astype(BF16), NT)
    posc = cols[:, 0:SUBLANE] * RUN_ALIGN + cols[:, SUBLANE:2 * SUBLANE]
    pos_ref[...] = jnp.concatenate([posc, jnp.zeros((RB, LANE - SUBLANE), F32)], axis=1).astype(jnp.int32)

    def run_copy(which):
        def make(row, slot, rows):
            return pltpu.make_async_copy(stage.at[which, pl.ds(row, rows)], xs_hbm.at[pl.ds(slot, rows)],
                                         sem.at[which])
        return make

    def pad_chunks(fn):
        def per_expert(e, carry):
            slot0 = pl.multiple_of(tab_ref[ROW_PAD0 * LANE + e], RUN_ALIGN)
            rows = pl.multiple_of(tab_ref[ROW_PADN * LANE + e] * RUN_ALIGN, RUN_ALIGN)

            @pl.when(rows > 0)
            def _():
                fn(pltpu.make_async_copy(zeros.at[pl.ds(0, rows)], xs_hbm.at[pl.ds(slot0, rows)], sem.at[2]))

            return carry

        lax.fori_loop(0, N_EXPERTS, per_expert, 0)

        def per_block(j, carry):
            fn(pltpu.make_async_copy(zeros, xs_hbm.at[pl.ds(pl.multiple_of(j * TME, TME), TME)], sem.at[2]))
            return carry

        lax.fori_loop(tab_ref[ROW_USED * LANE], N_EBLOCKS, per_block, 0)

    @pl.when(b == 0)
    def _():
        zeros[...] = jnp.zeros_like(zeros)
        pad_chunks(lambda cp: cp.start())

    @pl.when(b > 0)
    def _():
        make = run_copy(1 - buf)
        _block_runs(tab_ref, b - 1, lambda *run: make(*run).wait())

    make = run_copy(buf)
    _block_runs(tab_ref, b, lambda *run: make(*run).start())

    @pl.when(b == N_RB - 1)
    def _():
        _block_runs(tab_ref, b, lambda *run: make(*run).wait())

    @pl.when(b == 0)
    def _():
        pad_chunks(lambda cp: cp.wait())


def _expert_kernel(tab_ref, xs_ref, bgu_ref, bd_ref, wgu_hbm, wd_hbm, o_ref,
                   wgu_f32, wd_f32, wgu_bf, wd_bf, buf_ref, sem, *, layer):
    j = pl.program_id(0)
    n_used = tab_ref[ROW_USED * LANE]

    def block_expert(jj):
        return tab_ref[ROW_BE * LANE + jj]

    def weight_copies(e, buf):
        return (pltpu.make_async_copy(wgu_hbm.at[layer, e], wgu_f32.at[buf], sem.at[0, buf]),
                pltpu.make_async_copy(wd_hbm.at[layer, e], wd_f32.at[buf], sem.at[1, buf]))

    @pl.when(j < n_used)
    def _():
        e = block_expert(j)
        new_expert = (j == 0) | (e != block_expert(jnp.maximum(j - 1, 0)))

        @pl.when(j == 0)
        def _():
            buf_ref[0] = 1
            for cp in weight_copies(e, 0):
                cp.start()

        @pl.when(new_expert)
        def _():
            buf = 1 - buf_ref[0]
            buf_ref[0] = buf
            for cp in weight_copies(e, buf):
                cp.wait()
            wgu_bf[...] = wgu_f32[buf].astype(BF16)
            wd_bf[...] = wd_f32[buf].astype(BF16)
            nxt = tab_ref[ROW_EEND * LANE + e]

            @pl.when(nxt < n_used)
            def _():
                for cp in weight_copies(block_expert(nxt), 1 - buf):
                    cp.start()

        x = xs_ref[:, 0:D]
        slot_gate = xs_ref[:, D:D + 1].astype(F32) + xs_ref[:, D + LANE:D + LANE + 1].astype(F32)

        def gate_up(c):
            lo, hi = c * FF_CHUNK, (c + 1) * FF_CHUNK
            return (_dot(x, wgu_bf[:, lo:hi]) + bgu_ref[:, lo:hi],
                    _dot(x, wgu_bf[:, D_FF + lo:D_FF + hi]) + bgu_ref[:, D_FF + lo:D_FF + hi])

        y = None
        gu = gate_up(0)
        for c in range(D_FF // FF_CHUNK):
            nxt = gate_up(c + 1) if c + 1 < D_FF // FF_CHUNK else None
            gate = jnp.minimum(gu[0], SWIGLU_LIMIT)
            up = jnp.clip(gu[1], -SWIGLU_LIMIT, SWIGLU_LIMIT)
            act = ((up + 1.0) * (gate * _sigmoid(gate * SWIGLU_ALPHA))).astype(BF16)
            part = _dot(act, wd_bf[c * FF_CHUNK:(c + 1) * FF_CHUNK, :])
            y = part if y is None else y + part
            gu = nxt
        o_ref[...] = ((y + bd_ref[...]) * slot_gate).astype(BF16)

    @pl.when(j >= n_used)
    def _():
        o_ref[...] = jnp.zeros_like(o_ref)


def _combine_kernel(tab_ref, pos_ref, x_ref, gate_ref, fg_ref, yb_hbm, *rest, final):
    if final:
        prompt_ref, latent_ref, stage, sem = rest
    else:
        o_ref, stage, sem = rest
    b = pl.program_id(0)
    buf = b % 2

    def run_copy(which):
        def make(row, slot, rows):
            return pltpu.make_async_copy(yb_hbm.at[pl.ds(slot, rows)], stage.at[which, pl.ds(row, rows)],
                                         sem.at[which])
        return make

    @pl.when(b == 0)
    def _():
        stage[...] = jnp.zeros_like(stage)
        make = run_copy(0)
        _block_runs(tab_ref, 0, lambda *run: make(*run).start())

    @pl.when(b + 1 < N_RB)
    def _():
        make = run_copy(1 - buf)
        _block_runs(tab_ref, b + 1, lambda *run: make(*run).start())

    lane = lax.broadcasted_iota(jnp.int32, (RB, STAGE_ROWS), 1)
    pos = pos_ref[...]
    hit = lane == pos[:, 0:1]
    for kk in range(1, TOP_K):
        hit = hit | (lane == pos[:, kk:kk + 1])
    perm = jnp.where(hit, 1.0, 0.0).astype(BF16)
    make = run_copy(buf)
    _block_runs(tab_ref, b, lambda *run: make(*run).wait())
    out = x_ref[...] + gate_ref[...] * _dot(perm, stage[buf])
    if final:
        ms = jnp.mean(out * out, axis=-1, keepdims=True)
        out = out * lax.rsqrt(ms + NORM_EPS) * fg_ref[...]
        latent_ref[...] = out

        @pl.when(b < P_RB)
        def _():
            prompt_ref[...] = out
    else:
        o_ref[...] = out


def _moe_layer(x, mods, g2, router_w, router_b, w_gu, b_gu, w_down, b_down, layer, final_g):
    block_tab = pl.BlockSpec((None, SUBLANE, RB), lambda i: (i, 0, 0))
    hb, idx, gates, cnt = pl.pallas_call(
        _route_kernel,
        grid=(N_RB,),
        in_specs=[_row_spec(), _full_spec((1, D)), _mod_spec(3), _mod_spec(4), _full_spec((N_EXPERTS, D)),
                  _full_spec((N_EXPERTS, LANE))],
        out_specs=[_row_spec(), block_tab, block_tab, pl.BlockSpec((None, N_EXPERTS, LANE), lambda i: (i, 0, 0))],
        out_shape=[jax.ShapeDtypeStruct((N_TOK, D), BF16), jax.ShapeDtypeStruct((N_RB, SUBLANE, RB), jnp.int32),
                   jax.ShapeDtypeStruct((N_RB, SUBLANE, RB), F32),
                   jax.ShapeDtypeStruct((N_RB, N_EXPERTS, LANE), F32)],
        compiler_params=_params(),
        name="route",
    )(x, _row(g2), mods, mods, router_w.T, jnp.broadcast_to(router_b[:, None], (N_EXPERTS, LANE)))
    tab = pl.pallas_call(
        _plan_kernel,
        out_shape=jax.ShapeDtypeStruct((PLAN_ROWS, LANE), jnp.int32),
        name="plan",
    )(cnt).reshape(PLAN_ROWS * LANE)
    any_spec = pl.BlockSpec(memory_space=pl.ANY)
    xs, pos = pl.pallas_call(
        _dispatch_kernel,
        grid_spec=pltpu.PrefetchScalarGridSpec(
            num_scalar_prefetch=1,
            grid=(N_RB,),
            in_specs=[pl.BlockSpec((RB, D), lambda i, t: (i, 0)),
                      pl.BlockSpec((None, SUBLANE, RB), lambda i, t: (i, 0, 0)),
                      pl.BlockSpec((None, SUBLANE, RB), lambda i, t: (i, 0, 0))],
            out_specs=[any_spec, pl.BlockSpec((RB, LANE), lambda i, t: (i, 0))],
            scratch_shapes=[pltpu.VMEM((2, STAGE_ROWS, DX), BF16), pltpu.VMEM((TME, DX), BF16),
                            pltpu.SemaphoreType.DMA((3,))]),
        out_shape=[jax.ShapeDtypeStruct((N_SLOTS, DX), BF16), jax.ShapeDtypeStruct((N_TOK, LANE), jnp.int32)],
        compiler_params=_params(),
        name="dispatch",
    )(tab, hb, idx, gates)

    def used_block(j, t):
        return (jnp.minimum(j, t[ROW_USED * LANE] - 1), 0)

    def expert_row(j, t):
        return (layer, t[ROW_BE * LANE + j], 0, 0)

    yb = pl.pallas_call(
        functools.partial(_expert_kernel, layer=layer),
        grid_spec=pltpu.PrefetchScalarGridSpec(
            num_scalar_prefetch=1,
            grid=(N_EBLOCKS,),
            in_specs=[pl.BlockSpec((TME, DX), used_block),
                      pl.BlockSpec((None, None, 1, 2 * D_FF), expert_row),
                      pl.BlockSpec((None, None, 1, D), expert_row),
                      any_spec, any_spec],
            out_specs=pl.BlockSpec((TME, D), lambda j, t: (j, 0)),
            scratch_shapes=[pltpu.VMEM((2, D, 2 * D_FF), F32), pltpu.VMEM((2, D_FF, D), F32),
                            pltpu.VMEM((D, 2 * D_FF), BF16), pltpu.VMEM((D_FF, D), BF16),
                            pltpu.SMEM((1,), jnp.int32), pltpu.SemaphoreType.DMA((2, 2))]),
        out_shape=jax.ShapeDtypeStruct((N_SLOTS, D), BF16),
        compiler_params=_params(),
        name="experts",
    )(tab, xs, b_gu.reshape(DEPTH, N_EXPERTS, 1, 2 * D_FF), b_down.reshape(DEPTH, N_EXPERTS, 1, D),
      w_gu, w_down)
    final = final_g is not None
    fg = _row(final_g) if final else jnp.ones((1, D), F32)
    if final:
        out_specs = [pl.BlockSpec((RB, D), lambda i, t: (jnp.minimum(i, P_RB - 1), 0)),
                     pl.BlockSpec((RB, D), lambda i, t: (jnp.maximum(i - P_RB, 0), 0))]
        out_shape = [jax.ShapeDtypeStruct((N_PROMPT, D), F32), jax.ShapeDtypeStruct((N_SAMPLE, D), F32)]
    else:
        out_specs = [pl.BlockSpec((RB, D), lambda i, t: (i, 0))]
        out_shape = [jax.ShapeDtypeStruct((N_TOK, D), F32)]
    out = pl.pallas_call(
        functools.partial(_combine_kernel, final=final),
        grid_spec=pltpu.PrefetchScalarGridSpec(
            num_scalar_prefetch=1,
            grid=(N_RB,),
            in_specs=[pl.BlockSpec((RB, LANE), lambda i, t: (i, 0)), pl.BlockSpec((RB, D), lambda i, t: (i, 0)),
                      pl.BlockSpec((None, 1, D), lambda i, t: (_cond_of_block(i) * ADA_CHUNKS + 5, 0, 0)),
                      pl.BlockSpec((1, D), lambda i, t: (0, 0)), any_spec],
            out_specs=out_specs,
            scratch_shapes=[pltpu.VMEM((2, STAGE_ROWS, D), BF16), pltpu.SemaphoreType.DMA((2,))]),
        out_shape=out_shape,
        compiler_params=_params(),
        name="combine",
    )(tab, pos, x, mods, fg, yb)
    return out if final else out[0]


def kernel(x_prompt, x_sample, state_wkv_fwd, state_wkv_bwd, cache_k, cache_v, c, c_ctx, ada_w, ada_b, norm1_g, norm2_g, final_g, conv_w_in, conv_b_in, conv_w_dw, conv_b_dw, conv_ln_g, conv_ln_b, conv_w_out, conv_b_out, rwkv_mu, rwkv_w_r, rwkv_w_k, rwkv_w_v, rwkv_w_o, rwkv_w0, rwkv_w1, rwkv_w2, rwkv_a0, rwkv_a1, rwkv_a2, rwkv_g1, rwkv_g2, rwkv_k_k, rwkv_k_a, rwkv_r_k, rwkv_lnx_g, rwkv_lnx_b, attn_w_qkv, attn_w_o, attn_sinks, moe_router_w, moe_router_b, moe_w_gu, moe_b_gu, moe_w_down, moe_b_down):
    x = jnp.concatenate([x_prompt.reshape(N_PROMPT, D), x_sample.reshape(N_SAMPLE, D)], axis=0).astype(F32)
    cond = jnp.zeros((N_COND, D), F32).at[0].set(c_ctx).at[1:1 + S_BATCH].set(c)
    mods_all = _ada_table(cond, ada_w, ada_b)
    new_f, new_b, new_k, new_v = [], [], [], []
    ci = ri = ai = 0
    for layer in range(DEPTH):
        mods = mods_all[layer]
        kind = layer % 3
        if kind == 0:
            x = _conformer_layer(x, mods, norm1_g[layer], conv_w_in[ci], conv_b_in[ci], conv_w_dw[ci],
                                 conv_b_dw[ci], conv_ln_g[ci], conv_ln_b[ci], conv_w_out[ci], conv_b_out[ci])
            ci += 1
        elif kind == 1:
            x, s_f, s_b = _rwkv_layer(
                x, mods, norm1_g[layer], state_wkv_fwd[:, ri], state_wkv_bwd[:, ri], rwkv_mu[ri],
                rwkv_w_r[ri], rwkv_w_k[ri], rwkv_w_v[ri], rwkv_w_o[ri], rwkv_w0[ri].reshape(-1),
                rwkv_w1[ri], rwkv_w2[ri], rwkv_a0[ri].reshape(-1), rwkv_a1[ri], rwkv_a2[ri],
                rwkv_g1[ri], rwkv_g2[ri], rwkv_k_k[ri], rwkv_k_a[ri], rwkv_r_k[ri], rwkv_lnx_g[ri],
                rwkv_lnx_b[ri])
            new_f.append(s_f.astype(x_prompt.dtype))
            new_b.append(s_b.astype(x_prompt.dtype))
            ri += 1
        else:
            x, k_ctx, v_ctx = _attention_layer(x, mods, norm1_g[layer], cache_k[:, ai], cache_v[:, ai],
                                               attn_w_qkv[ai], attn_w_o[ai], attn_sinks[ai])
            new_k.append(k_ctx)
            new_v.append(v_ctx)
            ai += 1
        x = _moe_layer(x, mods, norm2_g[layer], moe_router_w[layer], moe_router_b[layer], moe_w_gu, moe_b_gu,
                       moe_w_down, moe_b_down, layer, final_g if layer == DEPTH - 1 else None)
    y_prompt = x[0].reshape(P_BATCH, P_SEQ, D)
    y_sample = x[1].reshape(S_BATCH, S_SEQ, D)
    return (y_prompt, y_sample, jnp.stack(new_f, axis=1), jnp.stack(new_b, axis=1),
            jnp.stack(new_k, axis=1), jnp.stack(new_v, axis=1))
```

```python
import functools

import numpy as np
import jax
import jax.numpy as jnp
from jax import lax
from jax.experimental import pallas as pl
from jax.experimental.pallas import tpu as pltpu

F32 = jnp.float32
BF16 = jnp.bfloat16

D = 1024
DEPTH = 4
P_BATCH, P_SEQ = 16, 256
S_BATCH, S_SEQ = 4, 2048
PAST = 512
GRID_W = 64
N_PROMPT = P_BATCH * P_SEQ
N_SAMPLE = S_BATCH * S_SEQ
N_TOK = N_PROMPT + N_SAMPLE
ADA_CHUNKS = 6
NORM_EPS = 1e-6
LN_EPS = 1e-5
CONV_WIDTH = 31
CONV_HALF = CONV_WIDTH // 2
HEAD = 64
N_HEAD = D // HEAD
GN_EPS = 64e-5
L2_EPS = 1e-12
N_KV = 4
GROUP = N_HEAD // N_KV
WINDOW = 128
ATTN_SCALE = HEAD ** -0.5
ROPE_THETA = 10000.0
ROPE_PAIRS = HEAD // 4
NEG_INF = -1e30
N_EXPERTS = 32
TOP_K = 4
D_FF = D
SWIGLU_LIMIT = 7.0
SWIGLU_ALPHA = 1.702

LANE = 128
SUBLANE = 8
MXU_DIM = 256
VMEM_LIMIT = 56 * 1024 * 1024

RB = 256
N_RB = N_TOK // RB
P_RB = N_PROMPT // RB
S_RB = S_SEQ // RB
N_COND = 8
CONV_HALO = 16
CONV_EXT = RB + 2 * CONV_HALO
SHIFT_HALO = SUBLANE
CHUNK = 64
N_CHUNK = N_TOK // CHUNK
HG = MXU_DIM
N_HG = D // HG
HEADS_PER_HG = HG // HEAD
N_SEQ = P_BATCH + S_BATCH
QB = 128
TME = 512
FF_CHUNK = 256
N_ASSIGN = N_TOK * TOP_K

NN = ((1,), (0,))
NT = ((1,), (1,))
TN = ((0,), (0,))


def _dot(a, b, dims=NN):
    return lax.dot_general(a, b, (dims, ((), ())), preferred_element_type=F32)


def _dot1(a, b, dims=NN):
    return _dot(a.astype(BF16), b.astype(BF16), dims)


def _split2(x):
    hi = x.astype(BF16)
    return hi, (x - hi.astype(F32)).astype(BF16)


def _dot3(a, b, dims=NN):
    ah, al = _split2(a)
    bh, bl = _split2(b)
    return _dot(ah, bh, dims) + (_dot(ah, bl, dims) + _dot(al, bh, dims))


def _sigmoid(x):
    return 1.0 / (1.0 + jnp.exp(-x))


def _rms_mod(x, g, shift, scale):
    ms = jnp.mean(x * x, axis=-1, keepdims=True)
    return (x * lax.rsqrt(ms + NORM_EPS) * g) * (1.0 + scale) + shift


def _cond_of_block(i):
    return jnp.where(i < P_RB, 0, 1 + (i - P_RB) // S_RB)


def _seq_pos(i):
    j = jnp.where(i < P_RB, 0, (i - P_RB) % S_RB)
    first = (i < P_RB) | (j == 0)
    last = (i < P_RB) | (j == S_RB - 1)
    return first, last


def _row_spec(width=D, rows=RB):
    return pl.BlockSpec((rows, width), lambda i: (i, 0))


def _mod_spec(chunk):
    return pl.BlockSpec((None, 1, D), lambda i: (_cond_of_block(i) * ADA_CHUNKS + chunk, 0, 0))


def _full_spec(shape):
    nd = len(shape)
    return pl.BlockSpec(shape, lambda i: (0,) * nd)


def _params(n_axes=1):
    return pltpu.CompilerParams(dimension_semantics=("arbitrary",) * n_axes,
                                vmem_limit_bytes=VMEM_LIMIT)


def _row(v):
    return v.reshape(1, -1)


def _ada_kernel(c_ref, w_ref, b_ref, o_ref):
    c = c_ref[...]
    o_ref[...] = _dot3(c * _sigmoid(c), w_ref[...]) + b_ref[...]


def _ada_table(cond, ada_w, ada_b):
    out = pl.pallas_call(
        _ada_kernel,
        grid=(DEPTH, ADA_CHUNKS),
        in_specs=[pl.BlockSpec((N_COND, D), lambda l, j: (0, 0)),
                  pl.BlockSpec((None, D, D), lambda l, j: (l, 0, j)),
                  pl.BlockSpec((None, 1, D), lambda l, j: (l, 0, j))],
        out_specs=pl.BlockSpec((None, N_COND, D), lambda l, j: (l, 0, j)),
        out_shape=jax.ShapeDtypeStruct((DEPTH, N_COND, ADA_CHUNKS * D), F32),
        compiler_params=_params(2),
        name="ada_table",
    )(cond, ada_w, ada_b.reshape(DEPTH, 1, ADA_CHUNKS * D))
    return out.reshape(DEPTH, N_COND * ADA_CHUNKS, 1, D)


def _stream_rows(x):
    if not isinstance(x, tuple):
        return (x,), [_row_spec()]
    return x, [pl.BlockSpec((RB, D), lambda i: (jnp.minimum(i, P_RB - 1), 0)),
               pl.BlockSpec((RB, D), lambda i: (jnp.maximum(i - P_RB, 0), 0))]


def _stream_block(x_refs):
    if len(x_refs) == 1:
        return x_refs[0][...]
    return jnp.where(pl.program_id(0) < P_RB, x_refs[0][...], x_refs[1][...])


def _conv_in_kernel(*refs, n_x):
    x_refs, (g_ref, sh_ref, sc_ref, w_ref, b_ref, u_ref) = refs[:n_x], refs[n_x:]
    h = _rms_mod(_stream_block(x_refs), g_ref[...], sh_ref[...], sc_ref[...]).astype(BF16)
    z = _dot(h, w_ref[...]) + b_ref[...]
    u_ref[...] = z[:, :D] * _sigmoid(z[:, D:])


def _conv_out_kernel(up_ref, uc_ref, un_ref, *refs, n_x):
    x_refs = refs[:n_x]
    wdw_ref, bdw_ref, lg_ref, lb_ref, wo_ref, bo_ref, gate_ref, o_ref, ext_ref, act_ref = refs[n_x:]
    first, last = _seq_pos(pl.program_id(0))
    ext_ref[0, 0:CONV_HALO, :] = jnp.where(first, 0.0, up_ref[...])
    ext_ref[0, CONV_HALO:CONV_HALO + RB, :] = uc_ref[...]
    ext_ref[0, CONV_HALO + RB:, :] = jnp.where(last, 0.0, un_ref[...])
    keep = CONV_EXT - SUBLANE
    for s in range(1, SUBLANE):
        ext_ref[s, 0:keep, :] = ext_ref[0, s:s + keep, :]
    rows = 128
    base = CONV_HALO - CONV_HALF
    for ct in range(D // LANE):
        lanes = slice(ct * LANE, (ct + 1) * LANE)
        for rc in range(RB // rows):
            acc = jnp.broadcast_to(bdw_ref[:, lanes], (rows, LANE))
            for t in range(CONV_WIDTH):
                s = (base + t) % SUBLANE
                r0 = rc * rows + base + t - s
                acc = acc + ext_ref[s, r0:r0 + rows, lanes] * wdw_ref[t:t + 1, lanes]
            act_ref[rc * rows:(rc + 1) * rows, lanes] = acc
    c = act_ref[...]
    mu = jnp.mean(c, axis=-1, keepdims=True)
    cc = c - mu
    var = jnp.mean(cc * cc, axis=-1, keepdims=True)
    y = cc * lax.rsqrt(var + LN_EPS) * lg_ref[...] + lb_ref[...]
    y = y * _sigmoid(y)
    out = _dot(y.astype(BF16), wo_ref[...]) + bo_ref[...]
    o_ref[...] = _stream_block(x_refs) + gate_ref[...] * out


def _conformer_layer(x, mods, g1, w_in, b_in, w_dw, b_dw, ln_g, ln_b, w_out, b_out):
    xs, x_specs = _stream_rows(x)
    u = pl.pallas_call(
        functools.partial(_conv_in_kernel, n_x=len(xs)),
        grid=(N_RB,),
        in_specs=x_specs + [_full_spec((1, D)), _mod_spec(0), _mod_spec(1),
                            _full_spec((D, 2 * D)), _full_spec((1, 2 * D))],
        out_specs=_row_spec(),
        out_shape=jax.ShapeDtypeStruct((N_TOK, D), F32),
        compiler_params=_params(),
        name="conv_in",
    )(*xs, _row(g1), mods, mods, w_in.astype(BF16), _row(b_in))
    hb = RB // CONV_HALO
    n_hb = N_TOK // CONV_HALO
    wdw = jnp.zeros((32, D), F32).at[:CONV_WIDTH].set(w_dw)
    return pl.pallas_call(
        functools.partial(_conv_out_kernel, n_x=len(xs)),
        grid=(N_RB,),
        in_specs=[pl.BlockSpec((CONV_HALO, D), lambda i: (jnp.maximum(i * hb - 1, 0), 0)),
                  _row_spec(),
                  pl.BlockSpec((CONV_HALO, D), lambda i: (jnp.minimum((i + 1) * hb, n_hb - 1), 0))]
        + x_specs + [_full_spec((32, D)), _full_spec((1, D)), _full_spec((1, D)),
                     _full_spec((1, D)), _full_spec((D, D)), _full_spec((1, D)), _mod_spec(2)],
        out_specs=_row_spec(),
        out_shape=jax.ShapeDtypeStruct((N_TOK, D), F32),
        scratch_shapes=[pltpu.VMEM((SUBLANE, CONV_EXT, D), F32), pltpu.VMEM((RB, D), F32)],
        compiler_params=_params(),
        name="conv_out",
    )(u, u, u, *xs, wdw, _row(b_dw), _row(ln_g), _row(ln_b), w_out.astype(BF16), _row(b_out), mods)


def _softplus(z):
    return jnp.maximum(z, 0.0) + jnp.log(1.0 + jnp.exp(-jnp.abs(z)))


def _head_pair_sums(x):
    left = lax.broadcasted_iota(jnp.int32, x.shape, 1) < HEAD
    sl = jnp.sum(jnp.where(left, x, 0.0), axis=-1, keepdims=True)
    sr = jnp.sum(jnp.where(left, 0.0, x), axis=-1, keepdims=True)
    return jnp.where(left, sl, sr)


def _rwkv_in_kernel(xp_ref, xc_ref, xn_ref, g_ref, sh_ref, sc_ref, mu_ref, wr_ref, wk_ref, wv_ref,
                    w1_ref, w2_ref, w0_ref, a1_ref, a2_ref, a0_ref, g1_ref, g2_ref, kkk_ref,
                    r_o, k_o, v_o, g_o, kk_o, wl0_o, wl1_o, as0_o, as1_o, ext_ref):
    first, last = _seq_pos(pl.program_id(0))
    g, sh, sc = g_ref[...], sh_ref[...], sc_ref[...]
    h = _rms_mod(xc_ref[...], g, sh, sc)
    hp = _rms_mod(xp_ref[SHIFT_HALO - 1:SHIFT_HALO, :], g, sh, sc)
    hn = _rms_mod(xn_ref[0:1, :], g, sh, sc)
    ext_ref[SHIFT_HALO - 1:SHIFT_HALO, :] = jnp.where(first, 0.0, hp)
    ext_ref[SHIFT_HALO:SHIFT_HALO + RB, :] = h
    ext_ref[SHIFT_HALO + RB:SHIFT_HALO + RB + 1, :] = jnp.where(last, 0.0, hn)
    xx = 0.5 * (ext_ref[SHIFT_HALO - 1:SHIFT_HALO - 1 + RB, :]
                + ext_ref[SHIFT_HALO + 1:SHIFT_HALO + 1 + RB, :]) - h

    def mix(j):
        return (h + xx * mu_ref[j:j + 1, :]).astype(BF16)

    r_o[...] = _dot(mix(0), wr_ref[...])
    k = _dot(mix(2), wk_ref[...])
    k_o[...] = k
    v_o[...] = _dot(mix(3), wv_ref[...])
    t1 = jnp.tanh(_dot(mix(1), w1_ref[...]))
    wl = w0_ref[...] + _dot(t1.astype(BF16), w2_ref[...])
    wlog = -_softplus(-wl) - 0.5
    wl0_o[...] = wlog[:, :D]
    wl1_o[...] = wlog[:, D:]
    ah = _dot(mix(4), a1_ref[...])
    asig = _sigmoid(a0_ref[...] + _dot(ah.astype(BF16), a2_ref[...]))
    as0_o[...] = asig[:, :D]
    as1_o[...] = asig[:, D:]
    gh = _sigmoid(_dot(mix(5), g1_ref[...]))
    g_o[...] = _dot(gh.astype(BF16), g2_ref[...])
    kr = k * kkk_ref[...]
    for ct in range(D // LANE):
        lanes = slice(ct * LANE, (ct + 1) * LANE)
        t = kr[:, lanes]
        kk_o[:, lanes] = t * lax.rsqrt(_head_pair_sums(t * t) + L2_EPS)


def _block_diag(x, mask):
    return jnp.where(mask, jnp.concatenate([x] * HEADS_PER_HG, axis=0), 0.0)


def _wkv_groups(S, rt, at, bt, kt, v, p_end, rev):
    ids = range(len(S))
    row = lax.broadcasted_iota(jnp.int32, (CHUNK, HG), 0)
    col = lax.broadcasted_iota(jnp.int32, (CHUNK, HG), 1) % CHUNK
    strict = [row < col if rev[i] else row > col for i in ids]
    incl = [row <= col if rev[i] else row >= col for i in ids]
    bdm = (lax.broadcasted_iota(jnp.int32, (HG, HG), 0) // CHUNK
           == lax.broadcasted_iota(jnp.int32, (HG, HG), 1) // CHUNK)
    bd = functools.partial(_block_diag, mask=bdm)

    def same_block(n):
        return (row // n) == (col // n)

    def each(fn, *lists):
        return [fn(*(l[i] for l in lists)) for i in ids]

    kb, kkb, vb = each(bd, bt), each(bd, kt), each(bd, v)
    ar = each(lambda a, r: jnp.concatenate([a, r], axis=0), at, rt)
    g_b = each(lambda x, y: _dot1(x, y, NT), ar, kb)
    g_k = each(lambda x, y: _dot1(x, y, NT), ar, kkb)
    g_s = each(lambda x, y: _dot1(x, y, NT), ar, S)
    a_ab = each(lambda m, g: jnp.where(m, g[:CHUNK], 0.0), strict, g_b)
    a_rb = each(lambda m, g: jnp.where(m, g[CHUNK:], 0.0), incl, g_b)
    a_ak = each(lambda m, g: jnp.where(m, g[:CHUNK], 0.0), strict, g_k)
    a_rk = each(lambda m, g: jnp.where(m, g[CHUNK:], 0.0), incl, g_k)
    wm = each(lambda g, a, w: g[:CHUNK] + _dot1(a, w), g_s, a_ak, vb)
    n1 = each(lambda a: jnp.where(same_block(4), a, 0.0), a_ab)
    n1b = each(bd, n1)
    n2 = each(_dot1, n1, n1b)
    n3 = each(_dot1, n2, n1b)
    eye = jnp.where(row == col, 1.0, 0.0)
    t = each(lambda a, b, c: eye + a + b + c, n1, n2, n3)
    for n in (8, 16, 32):
        lower = same_block(n) & ~same_block(n // 2)
        m = each(lambda a: jnp.where(lower, a, 0.0), a_ab)
        mt = each(_dot1, m, each(bd, t))
        t = each(lambda a, b: a + _dot1(a, b), t, each(bd, mt))
    m = each(lambda a: jnp.where(same_block(32), 0.0, a), a_ab)
    x = each(_dot1, t, each(bd, wm))
    mx = each(_dot1, m, each(bd, x))
    u = each(lambda a, b, c: a + _dot1(b, c), x, t, each(bd, mx))
    ub = each(bd, u)
    y = each(lambda g, a, b, c, d: g[CHUNK:] + _dot1(a, b) + _dot1(c, d), g_s, a_rb, ub, a_rk, vb)
    uv = each(lambda a, b: jnp.concatenate([a, b], axis=0), u, v)
    bk = each(lambda a, b: jnp.concatenate([a, b], axis=0), bt, kt)
    s_new = each(lambda s, a, b, p: (s + jnp.where(bdm, _dot1(a, b, TN), 0.0)) * p, S, uv, bk, p_end)
    return s_new, y


def _wkv_prepare(r_ref, k_ref, v_ref, kk_ref, wl_ref, a_ref, ka_ref, rk_ref, bonus_ref, reverse):
    lw = -jnp.exp(wl_ref[...])
    ti = lax.broadcasted_iota(jnp.int32, (CHUNK, CHUNK), 0)
    tj = lax.broadcasted_iota(jnp.int32, (CHUNK, CHUNK), 1)
    tri = jnp.where((ti <= tj) if reverse else (ti >= tj), 1.0, 0.0).astype(BF16)
    p0 = lw.astype(BF16)
    r1 = lw - p0.astype(F32)
    p1 = r1.astype(BF16)
    p2 = (r1 - p1.astype(F32)).astype(BF16)
    li = _dot(tri, p0) + (_dot(tri, p1) + _dot(tri, p2))
    e_incl = jnp.exp(li)
    e_neg = jnp.exp(-li)
    e_excl = jnp.exp(li - lw)
    p_end = jnp.exp(jnp.sum(lw, axis=0, keepdims=True))
    r, v, kk, asig = r_ref[...], v_ref[...], kk_ref[...], a_ref[...]
    kd = k_ref[...] * (1.0 + (asig - 1.0) * ka_ref[...])
    q = r * kd * rk_ref[...]
    for ct in range(D // LANE):
        lanes = slice(ct * LANE, (ct + 1) * LANE)
        bonus_ref[:, lanes] = _head_pair_sums(q[:, lanes]) * v[:, lanes]
    return r * e_incl, -kk * e_excl, kk * asig * e_neg, kd * e_neg, v, p_end


def _wkv_kernel(seq_ref, st_ref, en_ref,
                rf_ref, kf_ref, vf_ref, kkf_ref, wlf_ref, af_ref,
                rb_ref, kb_ref, vb_ref, kkb_ref, wlb_ref, ab_ref,
                ka_ref, rk_ref, s0f_ref, s0b_ref,
                yf_ref, bonf_ref, yb_ref, bonb_ref, sff_ref, sfb_ref, sf_scr, sb_scr):
    del seq_ref
    s = pl.program_id(0)
    rs = N_CHUNK - 1 - s

    @pl.when(s == 0)
    def _():
        sf_scr[...] = jnp.zeros_like(sf_scr)
        sb_scr[...] = jnp.zeros_like(sb_scr)

    bdm = (lax.broadcasted_iota(jnp.int32, (HG, HG), 0) // CHUNK
           == lax.broadcasted_iota(jnp.int32, (HG, HG), 1) // CHUNK)
    starts = (st_ref[s] == 1, en_ref[rs] == 1)
    prep = (_wkv_prepare(rf_ref, kf_ref, vf_ref, kkf_ref, wlf_ref, af_ref, ka_ref, rk_ref, bonf_ref, False),
            _wkv_prepare(rb_ref, kb_ref, vb_ref, kkb_ref, wlb_ref, ab_ref, ka_ref, rk_ref, bonb_ref, True))
    scr, s0, y_out, s_out = (sf_scr, sb_scr), (s0f_ref, s0b_ref), (yf_ref, yb_ref), (sff_ref, sfb_ref)
    args = [[] for _ in range(7)]
    rev = []
    for d in range(2):
        for gi in range(N_HG):
            lanes = slice(gi * HG, (gi + 1) * HG)
            args[0].append(jnp.where(starts[d], _block_diag(s0[d][gi], bdm), scr[d][gi]))
            for j in range(6):
                args[j + 1].append(prep[d][j][:, lanes])
            rev.append(d == 1)
    s_new, y = _wkv_groups(*args, rev)
    for d in range(2):
        for gi in range(N_HG):
            sg = s_new[d * N_HG + gi]
            scr[d][gi] = sg
            y_out[d][:, gi * HG:(gi + 1) * HG] = y[d * N_HG + gi]
            acc = sg[0:HEAD]
            for j in range(1, HEADS_PER_HG):
                acc = acc + sg[j * HEAD:(j + 1) * HEAD]
            s_out[d][gi] = acc


def _chunk_tables():
    seq = np.concatenate([np.repeat(np.arange(P_BATCH), P_SEQ // CHUNK),
                          P_BATCH + np.repeat(np.arange(S_BATCH), S_SEQ // CHUNK)])
    start = np.ones(N_CHUNK, np.int32)
    start[1:] = seq[1:] != seq[:-1]
    end = np.ones(N_CHUNK, np.int32)
    end[:-1] = seq[1:] != seq[:-1]
    return jnp.asarray(seq, jnp.int32), jnp.asarray(start, jnp.int32), jnp.asarray(end, jnp.int32)


def _wkv_scan(r, k, v, kk, wl_f, as_f, wl_b, as_b, k_a, r_k, s0f, s0b):
    fwd = pl.BlockSpec((CHUNK, D), lambda s, seq, st, en: (s, 0))
    bwd = pl.BlockSpec((CHUNK, D), lambda s, seq, st, en: (N_CHUNK - 1 - s, 0))
    par = pl.BlockSpec((1, D), lambda s, seq, st, en: (0, 0))
    sfw = pl.BlockSpec((None, N_HG, HEAD, HG), lambda s, seq, st, en: (seq[s], 0, 0, 0))
    sbw = pl.BlockSpec((None, N_HG, HEAD, HG), lambda s, seq, st, en: (seq[N_CHUNK - 1 - s], 0, 0, 0))
    tok = jax.ShapeDtypeStruct((N_TOK, D), F32)
    state = jax.ShapeDtypeStruct((N_SEQ, N_HG, HEAD, HG), F32)
    return pl.pallas_call(
        _wkv_kernel,
        grid_spec=pltpu.PrefetchScalarGridSpec(
            num_scalar_prefetch=3,
            grid=(N_CHUNK,),
            in_specs=[fwd] * 6 + [bwd] * 6 + [par, par, sfw, sbw],
            out_specs=[fwd, fwd, bwd, bwd, sfw, sbw],
            scratch_shapes=[pltpu.VMEM((N_HG, HG, HG), F32), pltpu.VMEM((N_HG, HG, HG), F32)]),
        out_shape=[tok, tok, tok, tok, state, state],
        compiler_params=_params(),
        name="wkv",
    )(*_chunk_tables(), r, k, v, kk, wl_f, as_f, r, k, v, kk, wl_b, as_b, k_a, r_k, s0f, s0b)


def _rwkv_out_kernel(yf_ref, yb_ref, bf_ref, bb_ref, g_ref, x_ref, lg_ref, lb_ref, wo_ref, gate_ref,
                     o_ref, act_ref):
    for ct in range(D // LANE):
        lanes = slice(ct * LANE, (ct + 1) * LANE)
        y = yf_ref[:, lanes] + yb_ref[:, lanes]
        mean = _head_pair_sums(y) * (1.0 / HEAD)
        yc = y - mean
        var = _head_pair_sums(yc * yc) * (1.0 / HEAD)
        yn = yc * lax.rsqrt(var + GN_EPS) * lg_ref[:, lanes] + lb_ref[:, lanes]
        act_ref[:, lanes] = ((yn + (bf_ref[:, lanes] + bb_ref[:, lanes])) * g_ref[:, lanes]).astype(BF16)
    o_ref[...] = x_ref[...] + gate_ref[...] * _dot(act_ref[...], wo_ref[...])


def _state_to_cat(s):
    b = s.shape[0]
    return s.reshape(b, N_HG, HEADS_PER_HG, HEAD, HEAD).transpose(0, 1, 3, 2, 4).reshape(b, N_HG, HEAD, HG)


def _cat_to_state(s):
    b = s.shape[0]
    return s.reshape(b, N_HG, HEAD, HEADS_PER_HG, HEAD).transpose(0, 1, 3, 2, 4).reshape(b, N_HEAD, HEAD, HEAD)


def _rwkv_layer(x, mods, g1, s0_f, s0_b, mu, w_r, w_k, w_v, w_o, w0, w1, w2, a0, a1, a2, gl1, gl2,
                k_k, k_a, r_k, lnx_g, lnx_b):
    lora = w1.shape[-1]
    w1c = jnp.concatenate([w1[0], w1[1]], axis=1).astype(BF16)
    a1c = jnp.concatenate([a1[0], a1[1]], axis=1).astype(BF16)
    zero = jnp.zeros((lora, D), F32)
    w2c = jnp.concatenate([jnp.concatenate([w2[0], zero], 1), jnp.concatenate([zero, w2[1]], 1)], 0).astype(BF16)
    a2c = jnp.concatenate([jnp.concatenate([a2[0], zero], 1), jnp.concatenate([zero, a2[1]], 1)], 0).astype(BF16)
    gpad = MXU_DIM - gl1.shape[1]
    g1p = jnp.pad(gl1, ((0, 0), (0, gpad))).astype(BF16)
    g2p = jnp.pad(gl2, ((0, gpad), (0, 0))).astype(BF16)
    mu8 = jnp.zeros((SUBLANE, D), F32).at[:6].set(mu)
    hb = RB // SHIFT_HALO
    n_hb = N_TOK // SHIFT_HALO
    tok = jax.ShapeDtypeStruct((N_TOK, D), F32)
    r, k, v, g, kk, wl0, wl1, as0, as1 = pl.pallas_call(
        _rwkv_in_kernel,
        grid=(N_RB,),
        in_specs=[pl.BlockSpec((SHIFT_HALO, D), lambda i: (jnp.maximum(i * hb - 1, 0), 0)),
                  _row_spec(),
                  pl.BlockSpec((SHIFT_HALO, D), lambda i: (jnp.minimum((i + 1) * hb, n_hb - 1), 0)),
                  _full_spec((1, D)), _mod_spec(0), _mod_spec(1), _full_spec((SUBLANE, D)),
                  _full_spec((D, D)), _full_spec((D, D)), _full_spec((D, D)),
                  _full_spec((D, 2 * lora)), _full_spec((2 * lora, 2 * D)), _full_spec((1, 2 * D)),
                  _full_spec((D, 2 * lora)), _full_spec((2 * lora, 2 * D)), _full_spec((1, 2 * D)),
                  _full_spec((D, MXU_DIM)), _full_spec((MXU_DIM, D)), _full_spec((1, D))],
        out_specs=[_row_spec()] * 9,
        out_shape=[tok] * 9,
        scratch_shapes=[pltpu.VMEM((RB + 2 * SHIFT_HALO, D), F32)],
        compiler_params=_params(),
        name="rwkv_in",
    )(x, x, x, _row(g1), mods, mods, mu8, w_r.astype(BF16), w_k.astype(BF16), w_v.astype(BF16),
      w1c, w2c, _row(w0), a1c, a2c, _row(a0), g1p, g2p, _row(k_k))
    zeros = jnp.zeros((P_BATCH, N_HEAD, HEAD, HEAD), F32)
    s0f = _state_to_cat(jnp.concatenate([zeros, s0_f.astype(F32)], axis=0))
    s0b = _state_to_cat(jnp.concatenate([zeros, s0_b.astype(F32)], axis=0))
    ka, rk = _row(k_a), _row(r_k)
    y_f, bonus_f, y_b, bonus_b, sf, sb = _wkv_scan(r, k, v, kk, wl0, as0, wl1, as1, ka, rk, s0f, s0b)
    x = pl.pallas_call(
        _rwkv_out_kernel,
        grid=(N_RB,),
        in_specs=[_row_spec()] * 6 + [_full_spec((1, D)), _full_spec((1, D)), _full_spec((D, D)),
                                      _mod_spec(2)],
        out_specs=_row_spec(),
        out_shape=tok,
        scratch_shapes=[pltpu.VMEM((RB, D), BF16)],
        compiler_params=_params(),
        name="rwkv_out",
    )(y_f, y_b, bonus_f, bonus_b, g, x, _row(lnx_g), _row(lnx_b), w_o.astype(BF16), mods)
    return x, _cat_to_state(sf[:P_BATCH]), _cat_to_state(sb[:P_BATCH])


KVW = N_KV * LANE
QKV_W = D + 2 * KVW


def _rope_tables():
    t = np.arange(S_SEQ)
    inv = ROPE_THETA ** (-np.arange(ROPE_PAIRS, dtype=np.float32) / ROPE_PAIRS)
    ang_r = (t // GRID_W).astype(np.float32)[:, None] * inv
    ang_c = (t % GRID_W).astype(np.float32)[:, None] * inv
    zero = np.zeros_like(ang_r)
    cos = np.concatenate([np.cos(ang_r), np.cos(ang_r), np.cos(ang_c), np.cos(ang_c)], 1)
    up = np.concatenate([-np.sin(ang_r), zero, -np.sin(ang_c), zero], 1)
    dn = np.concatenate([zero, np.sin(ang_r), zero, np.sin(ang_c)], 1)
    tile = lambda z: jnp.asarray(np.concatenate([z, z], 1), F32)
    return tile(cos), tile(up), tile(dn)


def _qkv_kernel(x_ref, g_ref, sh_ref, sc_ref, w_ref, cos_ref, up_ref, dn_ref, q_ref, k_ref, v_ref):
    h = _rms_mod(x_ref[...], g_ref[...], sh_ref[...], sc_ref[...]).astype(BF16)
    z = _dot(h, w_ref[...])
    v_ref[...] = z[:, D + KVW:]
    is_latent = pl.program_id(0) >= P_RB

    @pl.when(jnp.logical_not(is_latent))
    def _():
        q_ref[...] = z[:, :D]
        k_ref[...] = z[:, D:D + KVW]

    @pl.when(is_latent)
    def _():
        cos, up, dn = cos_ref[...], up_ref[...], dn_ref[...]
        for ct in range((D + KVW) // LANE):
            t = z[:, ct * LANE:(ct + 1) * LANE]
            rot = (t * cos + pltpu.roll(t, LANE - ROPE_PAIRS, 1) * up
                   + pltpu.roll(t, ROPE_PAIRS, 1) * dn)
            if ct < D // LANE:
                q_ref[:, ct * LANE:(ct + 1) * LANE] = rot
            else:
                k_ref[:, ct * LANE - D:(ct + 1) * LANE - D] = rot


def _stack_group_queries(q_ref, kv, rows):
    left = lax.broadcasted_iota(jnp.int32, (rows, LANE), 1) < HEAD
    blocks = []
    for pair in range(GROUP // 2):
        ct = kv * (GROUP // 2) + pair
        qt = q_ref[:, ct * LANE:(ct + 1) * LANE] * ATTN_SCALE
        blocks += [jnp.where(left, qt, 0.0), jnp.where(left, 0.0, qt)]
    return jnp.concatenate(blocks, axis=0).astype(BF16)


def _group_sinks(sink_ref, kv, rows):
    head = lax.broadcasted_iota(jnp.int32, (GROUP * rows, 1), 0) // rows
    out = jnp.zeros((GROUP * rows, 1), F32)
    for j in range(GROUP):
        out = jnp.where(head == j, sink_ref[0, kv * GROUP + j], out)
    return out


def _store_group_output(o_ref, kv, o, rows):
    left = lax.broadcasted_iota(jnp.int32, (rows, LANE), 1) < HEAD
    for pair in range(GROUP // 2):
        ct = kv * (GROUP // 2) + pair
        o_ref[:, ct * LANE:(ct + 1) * LANE] = jnp.where(
            left, o[2 * pair * rows:(2 * pair + 1) * rows], o[(2 * pair + 1) * rows:(2 * pair + 2) * rows]).astype(BF16)


def _ctx_attn_kernel(q_ref, k_ref, v_ref, sink_ref, o_ref):
    kvs = range(N_KV)
    q = [_stack_group_queries(q_ref, kv, P_SEQ) for kv in kvs]
    sink = [_group_sinks(sink_ref, kv, P_SEQ) for kv in kvs]
    s = [_dot(q[kv], k_ref[:, kv * LANE:(kv + 1) * LANE].astype(BF16), NT) for kv in kvs]
    m = [jnp.maximum(jnp.max(s[kv], axis=-1, keepdims=True), sink[kv]) for kv in kvs]
    p = [jnp.exp(s[kv] - m[kv]) for kv in kvs]
    den = [jnp.sum(p[kv], axis=-1, keepdims=True) + jnp.exp(sink[kv] - m[kv]) for kv in kvs]
    o = [_dot(p[kv].astype(BF16), v_ref[:, kv * LANE:(kv + 1) * LANE].astype(BF16)) / den[kv] for kv in kvs]
    for kv in kvs:
        _store_group_output(o_ref, kv, o[kv], P_SEQ)


def _lat_attn_kernel(q_ref, kp_ref, kc_ref, kn_ref, vp_ref, vc_ref, vn_ref, ck_ref, cv_ref, sink_ref, o_ref):
    n = pl.program_id(1)
    row = lax.broadcasted_iota(jnp.int32, (GROUP * QB, QB), 0) % QB
    col = lax.broadcasted_iota(jnp.int32, (GROUP * QB, QB), 1)
    ok_prev = (col >= row) & (n > 0)
    ok_next = (col <= row) & (n < S_SEQ // QB - 1)
    kvs = range(N_KV)

    def tile(ref, kv):
        return ref[:, kv * LANE:(kv + 1) * LANE].astype(BF16)

    def rmax(z):
        return jnp.max(z, axis=-1, keepdims=True)

    def rsum(z):
        return jnp.sum(z, axis=-1, keepdims=True)

    q = [_stack_group_queries(q_ref, kv, QB) for kv in kvs]
    sink = [_group_sinks(sink_ref, kv, QB) for kv in kvs]
    sp = [jnp.where(ok_prev, _dot(q[kv], tile(kp_ref, kv), NT), NEG_INF) for kv in kvs]
    sc = [_dot(q[kv], tile(kc_ref, kv), NT) for kv in kvs]
    sn = [jnp.where(ok_next, _dot(q[kv], tile(kn_ref, kv), NT), NEG_INF) for kv in kvs]
    sx = [_dot(q[kv], tile(ck_ref, kv), NT) for kv in kvs]
    m = [jnp.maximum(jnp.maximum(jnp.maximum(rmax(sp[kv]), rmax(sc[kv])), jnp.maximum(rmax(sn[kv]), rmax(sx[kv]))),
                     sink[kv]) for kv in kvs]
    pp = [jnp.exp(sp[kv] - m[kv]) for kv in kvs]
    pc = [jnp.exp(sc[kv] - m[kv]) for kv in kvs]
    pn = [jnp.exp(sn[kv] - m[kv]) for kv in kvs]
    px = [jnp.exp(sx[kv] - m[kv]) for kv in kvs]
    den = [rsum(pp[kv]) + rsum(pc[kv]) + rsum(pn[kv]) + rsum(px[kv]) + jnp.exp(sink[kv] - m[kv]) for kv in kvs]
    o = [(_dot(pp[kv].astype(BF16), tile(vp_ref, kv)) + _dot(pc[kv].astype(BF16), tile(vc_ref, kv))
          + _dot(pn[kv].astype(BF16), tile(vn_ref, kv)) + _dot(px[kv].astype(BF16), tile(cv_ref, kv))) / den[kv]
         for kv in kvs]
    for kv in kvs:
        _store_group_output(o_ref, kv, o[kv], QB)


def _proj_out_kernel(a_ref, x_ref, w_ref, gate_ref, o_ref):
    o_ref[...] = x_ref[...] + gate_ref[...] * _dot(a_ref[...], w_ref[...])


def _dup_heads(z):
    lead = z.shape[:-1]
    z = z.reshape(*lead, N_KV, 1, HEAD)
    return jnp.broadcast_to(z, (*lead, N_KV, 2, HEAD)).reshape(*lead, KVW)


def _attention_layer(x, mods, g1, cache_k, cache_v, w_qkv, w_o, sinks):
    nq = N_HEAD * HEAD
    nk = N_KV * HEAD
    wq, wk, wv = w_qkv[:, :nq], w_qkv[:, nq:nq + nk], w_qkv[:, nq + nk:]
    w_all = jnp.concatenate([wq, _dup_heads(wk), _dup_heads(wv)], axis=1).astype(BF16)
    cos, up, dn = _rope_tables()
    rope_spec = pl.BlockSpec((RB, LANE), lambda i: (jnp.where(i < P_RB, 0, (i - P_RB) % S_RB), 0))
    q, k, v = pl.pallas_call(
        _qkv_kernel,
        grid=(N_RB,),
        in_specs=[_row_spec(), _full_spec((1, D)), _mod_spec(0), _mod_spec(1), _full_spec((D, QKV_W)),
                  rope_spec, rope_spec, rope_spec],
        out_specs=[_row_spec(), _row_spec(KVW), _row_spec(KVW)],
        out_shape=[jax.ShapeDtypeStruct((N_TOK, D), F32), jax.ShapeDtypeStruct((N_TOK, KVW), F32),
                   jax.ShapeDtypeStruct((N_TOK, KVW), F32)],
        compiler_params=_params(),
        name="qkv",
    )(x, _row(g1), mods, mods, w_all, cos, up, dn)
    sink_row = _row(sinks.astype(F32))
    smem = pl.BlockSpec(memory_space=pltpu.SMEM)
    o_ctx = pl.pallas_call(
        _ctx_attn_kernel,
        grid=(P_BATCH,),
        in_specs=[_row_spec(D, P_SEQ), _row_spec(KVW, P_SEQ), _row_spec(KVW, P_SEQ), smem],
        out_specs=_row_spec(D, P_SEQ),
        out_shape=jax.ShapeDtypeStruct((N_PROMPT, D), BF16),
        compiler_params=_params(),
        name="ctx_attn",
    )(q, k, v, sink_row)
    nqb = S_SEQ // QB
    pq = N_PROMPT // QB

    def blk(off):
        def index(b, n):
            return (pq + b * nqb + jnp.clip(n + off, 0, nqb - 1), 0)
        return index

    ck = _dup_heads(cache_k.astype(F32).reshape(S_BATCH, PAST, nk))
    cv = _dup_heads(cache_v.astype(F32).reshape(S_BATCH, PAST, nk))
    ctx_spec = pl.BlockSpec((None, PAST, KVW), lambda b, n: (b, 0, 0))
    o_lat = pl.pallas_call(
        _lat_attn_kernel,
        grid=(S_BATCH, nqb),
        in_specs=[pl.BlockSpec((QB, D), blk(0)),
                  pl.BlockSpec((QB, KVW), blk(-1)), pl.BlockSpec((QB, KVW), blk(0)), pl.BlockSpec((QB, KVW), blk(1)),
                  pl.BlockSpec((QB, KVW), blk(-1)), pl.BlockSpec((QB, KVW), blk(0)), pl.BlockSpec((QB, KVW), blk(1)),
                  ctx_spec, ctx_spec, smem],
        out_specs=pl.BlockSpec((QB, D), lambda b, n: (b * nqb + n, 0)),
        out_shape=jax.ShapeDtypeStruct((N_SAMPLE, D), BF16),
        compiler_params=_params(2),
        name="lat_attn",
    )(q, k, k, k, v, v, v, ck, cv, sink_row)
    o = jnp.concatenate([o_ctx, o_lat], axis=0)
    x = pl.pallas_call(
        _proj_out_kernel,
        grid=(N_RB,),
        in_specs=[_row_spec(), _row_spec(), _full_spec((D, D)), _mod_spec(2)],
        out_specs=_row_spec(),
        out_shape=jax.ShapeDtypeStruct((N_TOK, D), F32),
        compiler_params=_params(),
        name="attn_out",
    )(o, x, w_o.astype(BF16), mods)
    new_k = k[:N_PROMPT].reshape(P_BATCH, P_SEQ, N_KV, 2, HEAD)[:, :, :, 0]
    new_v = v[:N_PROMPT].reshape(P_BATCH, P_SEQ, N_KV, 2, HEAD)[:, :, :, 0]
    return x, new_k, new_v


RUN_ALIGN = SUBLANE
STAGE_ROWS = -(-(RB * TOP_K + N_EXPERTS * (RUN_ALIGN - 1)) // MXU_DIM) * MXU_DIM
DX = D + LANE
N_EBLOCKS = -(-(N_ASSIGN + N_RB * N_EXPERTS * (RUN_ALIGN - 1) + N_EXPERTS * (TME - 1)) // TME)
N_SLOTS = N_EBLOCKS * TME
BE_ROWS = -(-N_EBLOCKS // LANE)
ROW_RS = 0
ROW_CNT = N_RB
ROW_EEND = 2 * N_RB
ROW_PAD0 = ROW_EEND + 1
ROW_PADN = ROW_EEND + 2
ROW_USED = ROW_EEND + 3
ROW_BE = ROW_EEND + SUBLANE
PLAN_ROWS = ROW_BE + SUBLANE


def _route_kernel(x_ref, g_ref, sh_ref, sc_ref, rwt_ref, rb_ref, h_ref, idx_ref, gate_ref, cnt_ref):
    h = _rms_mod(x_ref[...], g_ref[...], sh_ref[...], sc_ref[...])
    h_ref[...] = h.astype(BF16)
    logits = _dot3(rwt_ref[...], h, NT) + rb_ref[:, 0:1]
    erow = lax.broadcasted_iota(jnp.int32, logits.shape, 0)
    vals, idxs = [], []
    for _ in range(TOP_K):
        m = jnp.max(logits, axis=0, keepdims=True)
        ix = jnp.min(jnp.where(logits == m, erow, N_EXPERTS), axis=0, keepdims=True)
        vals.append(m)
        idxs.append(ix)
        logits = jnp.where(erow == ix, -3e38, logits)
    es = [jnp.exp(vv - vals[0]) for vv in vals]
    den = es[0] + es[1] + es[2] + es[3]
    srow = lax.broadcasted_iota(jnp.int32, (SUBLANE, RB), 0)
    idx_out = jnp.full((SUBLANE, RB), -1, jnp.int32)
    gate_out = jnp.zeros((SUBLANE, RB), F32)
    osum = jnp.zeros(logits.shape, F32)
    for kk in range(TOP_K):
        idx_out = jnp.where(srow == kk, idxs[kk], idx_out)
        gate_out = jnp.where(srow == kk, es[kk] / den, gate_out)
        osum = osum + jnp.where(erow == idxs[kk], 1.0, 0.0)
    idx_ref[...] = idx_out
    gate_ref[...] = gate_out
    cnt_ref[...] = jnp.broadcast_to(jnp.sum(osum, axis=1, keepdims=True), (N_EXPERTS, LANE))


def _ceil_to(x, m):
    return jnp.floor((x + (m - 1)) * (1.0 / m)) * m


def _plan_kernel(cnt_ref, tab_ref):
    ei = lax.broadcasted_iota(jnp.int32, (N_EXPERTS, LANE), 0)
    li = lax.broadcasted_iota(jnp.int32, (N_EXPERTS, LANE), 1)
    diag = ei == li

    def to_row(col):
        return jnp.sum(jnp.where(diag, col, 0.0), axis=0, keepdims=True).astype(jnp.int32)

    total = jnp.zeros((N_EXPERTS, LANE), F32)
    for b in range(N_RB):
        total = total + _ceil_to(cnt_ref[b], RUN_ALIGN)
    padded = _ceil_to(total, TME)
    tri = (lax.broadcasted_iota(jnp.int32, (N_EXPERTS, N_EXPERTS), 0)
           >= lax.broadcasted_iota(jnp.int32, (N_EXPERTS, N_EXPERTS), 1))
    bend = _dot(jnp.where(tri, 1.0, 0.0).astype(BF16), (padded * (1.0 / TME)).astype(BF16))
    pstart = bend * TME - padded
    acc = pstart
    for b in range(N_RB):
        c = cnt_ref[b]
        tab_ref[ROW_RS + b:ROW_RS + b + 1, :] = to_row(acc)
        tab_ref[ROW_CNT + b:ROW_CNT + b + 1, :] = to_row(c)
        acc = acc + _ceil_to(c, RUN_ALIGN)
    misc = jnp.concatenate([to_row(bend), to_row(pstart + total), to_row((padded - total) * (1.0 / RUN_ALIGN)),
                            bend[N_EXPERTS - 1:N_EXPERTS, :].astype(jnp.int32),
                            jnp.zeros((SUBLANE - 4, LANE), jnp.int32)], axis=0)
    tab_ref[ROW_EEND:ROW_EEND + SUBLANE, :] = misc
    rows = []
    for part in range(BE_ROWS):
        j = (li + part * LANE).astype(F32)
        be = jnp.sum(jnp.where(bend <= j, 1.0, 0.0), axis=0, keepdims=True)
        rows.append(jnp.minimum(be, N_EXPERTS - 1.0).astype(jnp.int32))
    rows.append(jnp.zeros((SUBLANE - BE_ROWS, LANE), jnp.int32))
    tab_ref[ROW_BE:ROW_BE + SUBLANE, :] = jnp.concatenate(rows, axis=0)


def _chunks(n):
    return lax.shift_right_logical(n + (RUN_ALIGN - 1), RUN_ALIGN.bit_length() - 1)


def _block_runs(tab_ref, blk, fn):
    def per_expert(e, row):
        rows = pl.multiple_of(_chunks(tab_ref[(ROW_CNT + blk) * LANE + e]) * RUN_ALIGN, RUN_ALIGN)

        @pl.when(rows > 0)
        def _():
            fn(pl.multiple_of(row, RUN_ALIGN), pl.multiple_of(tab_ref[(ROW_RS + blk) * LANE + e], RUN_ALIGN), rows)

        return row + rows

    lax.fori_loop(0, N_EXPERTS, per_expert, 0)


def _dispatch_kernel(tab_ref, h_ref, idx_ref, gate_ref, xs_hbm, pos_ref, stage, zeros, sem):
    b = pl.program_id(0)
    buf = b % 2
    erow = lax.broadcasted_iota(jnp.int32, (N_EXPERTS, RB), 0)
    ots = [jnp.where(erow == idx_ref[kk:kk + 1, :], 1.0, 0.0) for kk in range(TOP_K)]
    osum = ots[0] + ots[1] + ots[2] + ots[3]
    ti = lax.broadcasted_iota(jnp.int32, (RB, RB), 0)
    tj = lax.broadcasted_iota(jnp.int32, (RB, RB), 1)
    before = _dot(osum.astype(BF16), jnp.where(ti < tj, 1.0, 0.0).astype(BF16))
    cnt8 = _ceil_to(jnp.broadcast_to(jnp.sum(osum, axis=1, keepdims=True), (N_EXPERTS, LANE)), RUN_ALIGN)
    ei = lax.broadcasted_iota(jnp.int32, (N_EXPERTS, N_EXPERTS), 0)
    ej = lax.broadcasted_iota(jnp.int32, (N_EXPERTS, N_EXPERTS), 1)
    run0 = _dot(jnp.where(ei > ej, 1.0, 0.0).astype(BF16), cnt8.astype(BF16))[:, 0:1]
    where = run0 + before
    pos = [jnp.sum(ots[kk] * where, axis=0, keepdims=True) for kk in range(TOP_K)]
    jrow = lax.broadcasted_iota(jnp.int32, (STAGE_ROWS, RB), 0).astype(F32)
    hits = [jrow == pos[kk] for kk in range(TOP_K)]
    perm = jnp.where(hits[0] | hits[1] | hits[2] | hits[3], 1.0, 0.0).astype(BF16)
    gsum = jnp.zeros((STAGE_ROWS, 1), F32)
    for kk in range(TOP_K):
        gsum = gsum + jnp.sum(jnp.where(hits[kk], gate_ref[kk:kk + 1, :], 0.0), axis=1, keepdims=True)
    stage[buf, :, 0:D] = _dot(perm, h_ref[...])
    stage[buf, :, D:DX] = jnp.broadcast_to(gsum, (STAGE_ROWS, LANE))
    srow = lax.broadcasted_iota(jnp.int32, (2 * SUBLANE, RB), 0)
    parts = jnp.zeros((2 * SUBLANE, RB), F32)
    for kk in range(TOP_K):
        hi = jnp.floor(pos[kk] * (1.0 / RUN_ALIGN))
        parts = jnp.where(srow == kk, hi, parts)
        parts = jnp.where(srow == SUBLANE + kk, pos[kk] - hi * RUN_ALIGN, parts)
    cols = _dot(jnp.where(ti == tj, 1.0, 0.0).astype(BF16), parts.astype(BF16), NT)
    posc = cols[:, 0:SUBLANE] * RUN_ALIGN + cols[:, SUBLANE:2 * SUBLANE]
    pos_ref[...] = jnp.concatenate([posc, jnp.zeros((RB, LANE - SUBLANE), F32)], axis=1).astype(jnp.int32)

    def run_copy(which):
        def make(row, slot, rows):
            return pltpu.make_async_copy(stage.at[which, pl.ds(row, rows)], xs_hbm.at[pl.ds(slot, rows)],
                                         sem.at[which])
        return make

    def pad_chunks(fn):
        def per_expert(e, carry):
            slot0 = pl.multiple_of(tab_ref[ROW_PAD0 * LANE + e], RUN_ALIGN)
            rows = pl.multiple_of(tab_ref[ROW_PADN * LANE + e] * RUN_ALIGN, RUN_ALIGN)

            @pl.when(rows > 0)
            def _():
                fn(pltpu.make_async_copy(zeros.at[pl.ds(0, rows)], xs_hbm.at[pl.ds(slot0, rows)], sem.at[2]))

            return carry

        lax.fori_loop(0, N_EXPERTS, per_expert, 0)

        def per_block(j, carry):
            fn(pltpu.make_async_copy(zeros, xs_hbm.at[pl.ds(pl.multiple_of(j * TME, TME), TME)], sem.at[2]))
            return carry

        lax.fori_loop(tab_ref[ROW_USED * LANE], N_EBLOCKS, per_block, 0)

    @pl.when(b == 0)
    def _():
        zeros[...] = jnp.zeros_like(zeros)
        pad_chunks(lambda cp: cp.start())

    @pl.when(b > 0)
    def _():
        make = run_copy(1 - buf)
        _block_runs(tab_ref, b - 1, lambda *run: make(*run).wait())

    make = run_copy(buf)
    _block_runs(tab_ref, b, lambda *run: make(*run).start())

    @pl.when(b == N_RB - 1)
    def _():
        _block_runs(tab_ref, b, lambda *run: make(*run).wait())

    @pl.when(b == 0)
    def _():
        pad_chunks(lambda cp: cp.wait())


def _expert_kernel(tab_ref, xs_ref, bgu_ref, bd_ref, wgu_hbm, wd_hbm, o_ref,
                   wgu_f32, wd_f32, wgu_bf, wd_bf, buf_ref, sem, *, layer):
    j = pl.program_id(0)
    n_used = tab_ref[ROW_USED * LANE]

    def block_expert(jj):
        return tab_ref[ROW_BE * LANE + jj]

    def weight_copies(e, buf):
        return (pltpu.make_async_copy(wgu_hbm.at[layer, e], wgu_f32.at[buf], sem.at[0, buf]),
                pltpu.make_async_copy(wd_hbm.at[layer, e], wd_f32.at[buf], sem.at[1, buf]))

    @pl.when(j < n_used)
    def _():
        e = block_expert(j)
        new_expert = (j == 0) | (e != block_expert(jnp.maximum(j - 1, 0)))

        @pl.when(j == 0)
        def _():
            buf_ref[0] = 1
            for cp in weight_copies(e, 0):
                cp.start()

        @pl.when(new_expert)
        def _():
            buf = 1 - buf_ref[0]
            buf_ref[0] = buf
            for cp in weight_copies(e, buf):
                cp.wait()
            wgu_bf[...] = wgu_f32[buf].astype(BF16)
            wd_bf[...] = wd_f32[buf].astype(BF16)
            nxt = tab_ref[ROW_EEND * LANE + e]

            @pl.when(nxt < n_used)
            def _():
                for cp in weight_copies(block_expert(nxt), 1 - buf):
                    cp.start()

        x = xs_ref[:, 0:D].astype(BF16)

        def gate_up(c):
            lo, hi = c * FF_CHUNK, (c + 1) * FF_CHUNK
            return (_dot(x, wgu_bf[:, lo:hi]) + bgu_ref[:, lo:hi],
                    _dot(x, wgu_bf[:, D_FF + lo:D_FF + hi]) + bgu_ref[:, D_FF + lo:D_FF + hi])

        y = None
        gu = gate_up(0)
        for c in range(D_FF // FF_CHUNK):
            nxt = gate_up(c + 1) if c + 1 < D_FF // FF_CHUNK else None
            gate = jnp.minimum(gu[0], SWIGLU_LIMIT)
            up = jnp.clip(gu[1], -SWIGLU_LIMIT, SWIGLU_LIMIT)
            act = ((up + 1.0) * (gate * _sigmoid(gate * SWIGLU_ALPHA))).astype(BF16)
            part = _dot(act, wd_bf[c * FF_CHUNK:(c + 1) * FF_CHUNK, :])
            y = part if y is None else y + part
            gu = nxt
        o_ref[...] = (y + bd_ref[...]) * xs_ref[:, D:D + 1]

    @pl.when(j >= n_used)
    def _():
        o_ref[...] = jnp.zeros_like(o_ref)


def _combine_kernel(tab_ref, pos_ref, x_ref, gate_ref, fg_ref, yb_hbm, *rest, final):
    if final:
        prompt_ref, latent_ref, stage, sem = rest
    else:
        o_ref, stage, sem = rest
    b = pl.program_id(0)
    buf = b % 2

    def run_copy(which):
        def make(row, slot, rows):
            return pltpu.make_async_copy(yb_hbm.at[pl.ds(slot, rows)], stage.at[which, pl.ds(row, rows)],
                                         sem.at[which])
        return make

    @pl.when(b == 0)
    def _():
        stage[...] = jnp.zeros_like(stage)
        make = run_copy(0)
        _block_runs(tab_ref, 0, lambda *run: make(*run).start())

    @pl.when(b + 1 < N_RB)
    def _():
        make = run_copy(1 - buf)
        _block_runs(tab_ref, b + 1, lambda *run: make(*run).start())

    lane = lax.broadcasted_iota(jnp.int32, (RB, STAGE_ROWS), 1)
    pos = pos_ref[...]
    hit = lane == pos[:, 0:1]
    for kk in range(1, TOP_K):
        hit = hit | (lane == pos[:, kk:kk + 1])
    perm = jnp.where(hit, 1.0, 0.0).astype(BF16)
    make = run_copy(buf)
    _block_runs(tab_ref, b, lambda *run: make(*run).wait())
    out = x_ref[...] + gate_ref[...] * _dot(perm, stage[buf].astype(BF16))
    if final:
        ms = jnp.mean(out * out, axis=-1, keepdims=True)
        out = out * lax.rsqrt(ms + NORM_EPS) * fg_ref[...]
        latent_ref[...] = out

        @pl.when(b < P_RB)
        def _():
            prompt_ref[...] = out
    else:
        o_ref[...] = out


def _moe_layer(x, mods, g2, router_w, router_b, w_gu, b_gu, w_down, b_down, layer, final_g):
    block_tab = pl.BlockSpec((None, SUBLANE, RB), lambda i: (i, 0, 0))
    hb, idx, gates, cnt = pl.pallas_call(
        _route_kernel,
        grid=(N_RB,),
        in_specs=[_row_spec(), _full_spec((1, D)), _mod_spec(3), _mod_spec(4), _full_spec((N_EXPERTS, D)),
                  _full_spec((N_EXPERTS, LANE))],
        out_specs=[_row_spec(), block_tab, block_tab, pl.BlockSpec((None, N_EXPERTS, LANE), lambda i: (i, 0, 0))],
        out_shape=[jax.ShapeDtypeStruct((N_TOK, D), BF16), jax.ShapeDtypeStruct((N_RB, SUBLANE, RB), jnp.int32),
                   jax.ShapeDtypeStruct((N_RB, SUBLANE, RB), F32),
                   jax.ShapeDtypeStruct((N_RB, N_EXPERTS, LANE), F32)],
        compiler_params=_params(),
        name="route",
    )(x, _row(g2), mods, mods, router_w.T, jnp.broadcast_to(router_b[:, None], (N_EXPERTS, LANE)))
    tab = pl.pallas_call(
        _plan_kernel,
        out_shape=jax.ShapeDtypeStruct((PLAN_ROWS, LANE), jnp.int32),
        name="plan",
    )(cnt).reshape(PLAN_ROWS * LANE)
    any_spec = pl.BlockSpec(memory_space=pl.ANY)
    xs, pos = pl.pallas_call(
        _dispatch_kernel,
        grid_spec=pltpu.PrefetchScalarGridSpec(
            num_scalar_prefetch=1,
            grid=(N_RB,),
            in_specs=[pl.BlockSpec((RB, D), lambda i, t: (i, 0)),
                      pl.BlockSpec((None, SUBLANE, RB), lambda i, t: (i, 0, 0)),
                      pl.BlockSpec((None, SUBLANE, RB), lambda i, t: (i, 0, 0))],
            out_specs=[any_spec, pl.BlockSpec((RB, LANE), lambda i, t: (i, 0))],
            scratch_shapes=[pltpu.VMEM((2, STAGE_ROWS, DX), F32), pltpu.VMEM((TME, DX), F32),
                            pltpu.SemaphoreType.DMA((3,))]),
        out_shape=[jax.ShapeDtypeStruct((N_SLOTS, DX), F32), jax.ShapeDtypeStruct((N_TOK, LANE), jnp.int32)],
        compiler_params=_params(),
        name="dispatch",
    )(tab, hb, idx, gates)

    def used_block(j, t):
        return (jnp.minimum(j, t[ROW_USED * LANE] - 1), 0)

    def expert_row(j, t):
        return (layer, t[ROW_BE * LANE + j], 0, 0)

    yb = pl.pallas_call(
        functools.partial(_expert_kernel, layer=layer),
        grid_spec=pltpu.PrefetchScalarGridSpec(
            num_scalar_prefetch=1,
            grid=(N_EBLOCKS,),
            in_specs=[pl.BlockSpec((TME, DX), used_block),
                      pl.BlockSpec((None, None, 1, 2 * D_FF), expert_row),
                      pl.BlockSpec((None, None, 1, D), expert_row),
                      any_spec, any_spec],
            out_specs=pl.BlockSpec((TME, D), lambda j, t: (j, 0)),
            scratch_shapes=[pltpu.VMEM((2, D, 2 * D_FF), F32), pltpu.VMEM((2, D_FF, D), F32),
                            pltpu.VMEM((D, 2 * D_FF), BF16), pltpu.VMEM((D_FF, D), BF16),
                            pltpu.SMEM((1,), jnp.int32), pltpu.SemaphoreType.DMA((2, 2))]),
        out_shape=jax.ShapeDtypeStruct((N_SLOTS, D), F32),
        compiler_params=_params(),
        name="experts",
    )(tab, xs, b_gu.reshape(DEPTH, N_EXPERTS, 1, 2 * D_FF), b_down.reshape(DEPTH, N_EXPERTS, 1, D),
      w_gu, w_down)
    final = final_g is not None
    fg = _row(final_g) if final else jnp.ones((1, D), F32)
    if final:
        out_specs = [pl.BlockSpec((RB, D), lambda i, t: (jnp.minimum(i, P_RB - 1), 0)),
                     pl.BlockSpec((RB, D), lambda i, t: (jnp.maximum(i - P_RB, 0), 0))]
        out_shape = [jax.ShapeDtypeStruct((N_PROMPT, D), F32), jax.ShapeDtypeStruct((N_SAMPLE, D), F32)]
    else:
        out_specs = [pl.BlockSpec((RB, D), lambda i, t: (i, 0))]
        out_shape = [jax.ShapeDtypeStruct((N_TOK, D), F32)]
    out = pl.pallas_call(
        functools.partial(_combine_kernel, final=final),
        grid_spec=pltpu.PrefetchScalarGridSpec(
            num_scalar_prefetch=1,
            grid=(N_RB,),
            in_specs=[pl.BlockSpec((RB, LANE), lambda i, t: (i, 0)), pl.BlockSpec((RB, D), lambda i, t: (i, 0)),
                      pl.BlockSpec((None, 1, D), lambda i, t: (_cond_of_block(i) * ADA_CHUNKS + 5, 0, 0)),
                      pl.BlockSpec((1, D), lambda i, t: (0, 0)), any_spec],
            out_specs=out_specs,
            scratch_shapes=[pltpu.VMEM((2, STAGE_ROWS, D), F32), pltpu.SemaphoreType.DMA((2,))]),
        out_shape=out_shape,
        compiler_params=_params(),
        name="combine",
    )(tab, pos, x, mods, fg, yb)
    return out if final else out[0]


def kernel(x_prompt, x_sample, state_wkv_fwd, state_wkv_bwd, cache_k, cache_v, c, c_ctx, ada_w, ada_b, norm1_g, norm2_g, final_g, conv_w_in, conv_b_in, conv_w_dw, conv_b_dw, conv_ln_g, conv_ln_b, conv_w_out, conv_b_out, rwkv_mu, rwkv_w_r, rwkv_w_k, rwkv_w_v, rwkv_w_o, rwkv_w0, rwkv_w1, rwkv_w2, rwkv_a0, rwkv_a1, rwkv_a2, rwkv_g1, rwkv_g2, rwkv_k_k, rwkv_k_a, rwkv_r_k, rwkv_lnx_g, rwkv_lnx_b, attn_w_qkv, attn_w_o, attn_sinks, moe_router_w, moe_router_b, moe_w_gu, moe_b_gu, moe_w_down, moe_b_down):
    x = (x_prompt.reshape(N_PROMPT, D).astype(F32), x_sample.reshape(N_SAMPLE, D).astype(F32))
    cond = jnp.zeros((N_COND, D), F32).at[0].set(c_ctx).at[1:1 + S_BATCH].set(c)
    mods_all = _ada_table(cond, ada_w, ada_b)
    new_f, new_b, new_k, new_v = [], [], [], []
    ci = ri = ai = 0
    for layer in range(DEPTH):
        mods = mods_all[layer]
        kind = layer % 3
        if kind == 0:
            x = _conformer_layer(x, mods, norm1_g[layer], conv_w_in[ci], conv_b_in[ci], conv_w_dw[ci],
                                 conv_b_dw[ci], conv_ln_g[ci], conv_ln_b[ci], conv_w_out[ci], conv_b_out[ci])
            ci += 1
        elif kind == 1:
            x, s_f, s_b = _rwkv_layer(
                x, mods, norm1_g[layer], state_wkv_fwd[:, ri], state_wkv_bwd[:, ri], rwkv_mu[ri],
                rwkv_w_r[ri], rwkv_w_k[ri], rwkv_w_v[ri], rwkv_w_o[ri], rwkv_w0[ri].reshape(-1),
                rwkv_w1[ri], rwkv_w2[ri], rwkv_a0[ri].reshape(-1), rwkv_a1[ri], rwkv_a2[ri],
                rwkv_g1[ri], rwkv_g2[ri], rwkv_k_k[ri], rwkv_k_a[ri], rwkv_r_k[ri], rwkv_lnx_g[ri],
                rwkv_lnx_b[ri])
            new_f.append(s_f.astype(x_prompt.dtype))
            new_b.append(s_b.astype(x_prompt.dtype))
            ri += 1
        else:
            x, k_ctx, v_ctx = _attention_layer(x, mods, norm1_g[layer], cache_k[:, ai], cache_v[:, ai],
                                               attn_w_qkv[ai], attn_w_o[ai], attn_sinks[ai])
            new_k.append(k_ctx)
            new_v.append(v_ctx)
            ai += 1
        x = _moe_layer(x, mods, norm2_g[layer], moe_router_w[layer], moe_router_b[layer], moe_w_gu, moe_b_gu,
                       moe_w_down, moe_b_down, layer, final_g if layer == DEPTH - 1 else None)
    y_prompt = x[0].reshape(P_BATCH, P_SEQ, D)
    y_sample = x[1].reshape(S_BATCH, S_SEQ, D)
    return (y_prompt, y_sample, jnp.stack(new_f, axis=1), jnp.stack(new_b, axis=1),
            jnp.stack(new_k, axis=1), jnp.stack(new_v, axis=1))
```
